```python
import math
import jax
import jax.numpy as jnp
from jax import lax
import numpy as np

D_MODEL = 4096
BATCH = 1
SEQ = 8192
DEPTH = 4
DEC_BATCH = 4
DEC_SEQ = 4096
PAST_LEN = 128

GROUP_WIDTH = D_MODEL // 4
MIX_WIDTH = 4 * GROUP_WIDTH
DIFF_HEAD_DIM = 128
DIFF_HEADS = GROUP_WIDTH // (2 * DIFF_HEAD_DIM)
REL_BUCKETS = 32
REL_MAX_DIST = 128
S5_CH = 16
S5_GROUPS = GROUP_WIDTH // S5_CH
S5_STATE = 64
S5_DT_MIN = 1e-3
S5_DT_MAX = 1e-1
LRU_BLOCK = 256
LRU_BLOCKS = GROUP_WIDTH // LRU_BLOCK
LRU_CONV = 4
LRU_C = 8.0
MLA_NOPE = 128
MLA_ROPE = 64
MLA_V = 128
MLA_HEADS = GROUP_WIDTH // MLA_V
MLA_Q_RANK = 3 * D_MODEL // 8
MLA_KV_RANK = D_MODEL // 8
ROPE_BASE = 10000.0
D_FF = 4 * D_MODEL
FFN_CONV = 3
Q_BLOCK = 128
EPS = 1e-6

N_DIFF = 3 * GROUP_WIDTH
N_S5 = GROUP_WIDTH
N_LRU = 2 * GROUP_WIDTH
N_MLA = MLA_Q_RANK + MLA_KV_RANK + MLA_ROPE
N_IN = N_DIFF + N_S5 + N_LRU + N_MLA
SPLITS = [N_DIFF, N_DIFF + N_S5, N_DIFF + N_S5 + N_LRU]

kernel_name = 'hybrid_bidir_encoder'

F32 = jnp.float32


def rmsnorm(x, g):
    xf = x.astype(F32)
    y = xf * lax.rsqrt(jnp.mean(xf * xf, axis=-1, keepdims=True) + EPS)
    return (y * g.astype(F32)).astype(x.dtype)


def dwconv_centred(x, w, b):
    K = w.shape[0]
    L = x.shape[1]
    left = K // 2
    xp = jnp.pad(x, ((0, 0), (left, K - 1 - left), (0, 0)))
    y = xp[:, 0:L] * w[0]
    for k in range(1, K):
        y = y + xp[:, k:k + L] * w[k]
    return y + b


def rope(x, pos):
    half = x.shape[-1] // 2
    inv = ROPE_BASE ** (-jnp.arange(half, dtype=F32) / half)
    ang = pos.astype(F32)[:, None] * inv[None, :]
    cos, sin = jnp.cos(ang), jnp.sin(ang)
    xf = x.astype(F32)
    x1, x2 = xf[..., :half], xf[..., half:]
    return jnp.concatenate([x1 * cos - x2 * sin, x1 * sin + x2 * cos], axis=-1).astype(x.dtype)


def rel_bucket(rel):
    half = REL_BUCKETS // 2
    exact = half // 2
    n = jnp.abs(rel)
    sign = jnp.where(rel > 0, half, 0)
    nf = jnp.maximum(n, exact).astype(F32)
    large = exact + (jnp.log(nf / exact) / math.log(REL_MAX_DIST / exact) * (half - exact)).astype(jnp.int32)
    large = jnp.minimum(large, half - 1)
    return sign + jnp.where(n < exact, n, large)


def diff_attention(z, lam_vecs, subln_g, rel_table, lam_init):
    bsz, L, _ = z.shape
    q, k, v = jnp.split(z, 3, axis=-1)
    q = q.reshape(bsz, L, DIFF_HEADS, 2, DIFF_HEAD_DIM).transpose(3, 0, 2, 1, 4)
    k = k.reshape(bsz, L, DIFF_HEADS, 2, DIFF_HEAD_DIM).transpose(3, 0, 2, 1, 4)
    v = v.reshape(bsz, L, DIFF_HEADS, 2 * DIFF_HEAD_DIM).transpose(0, 2, 1, 3)
    lf = lam_vecs.astype(F32)
    lam = jnp.exp(jnp.sum(lf[0] * lf[1])) - jnp.exp(jnp.sum(lf[2] * lf[3])) + lam_init
    qb = min(Q_BLOCK, L)
    nb = L // qb
    qblocks = q.reshape(2, bsz, DIFF_HEADS, nb, qb, DIFF_HEAD_DIM).transpose(3, 0, 1, 2, 4, 5)
    kpos = jnp.arange(L)
    scale = DIFF_HEAD_DIM ** -0.5

    def block(args):
        i, qi = args
        qpos = i * qb + jnp.arange(qb)
        bias = rel_table[rel_bucket(kpos[None, :] - qpos[:, None])].astype(F32).transpose(2, 0, 1)
        s = jnp.einsum('nbhqd,nbhkd->nbhqk', qi, k, preferred_element_type=F32) * scale + bias
        p = jax.nn.softmax(s, axis=-1)
        pd = p[0] - lam * p[1]
        return jnp.einsum('bhqk,bhkv->bhqv', pd.astype(v.dtype), v)

    o = lax.map(block, (jnp.arange(nb), qblocks))
    o = o.transpose(1, 0, 3, 2, 4).reshape(bsz, L, DIFF_HEADS, 2 * DIFF_HEAD_DIM)
    o = rmsnorm(o, subln_g) * (1.0 - lam_init)
    return o.reshape(bsz, L, GROUP_WIDTH).astype(z.dtype)


def _complex_linear_combine(e1, e2):
    a1r, a1i, b1r, b1i = e1
    a2r, a2i, b2r, b2i = e2
    return (a1r * a2r - a1i * a2i, a1r * a2i + a1i * a2r,
            a2r * b1r - a2i * b1i + b2r, a2r * b1i + a2i * b1r + b2i)


def s5_mixer(u, lam_re, lam_im, log_dt, b_re, b_im, c_re, c_im, d_skip, w_glu):
    bsz, L, _ = u.shape
    uf = u.astype(F32).reshape(bsz, L, S5_GROUPS, S5_CH)
    dt = jnp.exp(log_dt.astype(F32))[..., None]
    lr = lam_re.astype(F32)
    li = lam_im.astype(F32)
    mag = jnp.exp(lr * dt)
    abar_re = mag * jnp.cos(li * dt)
    abar_im = mag * jnp.sin(li * dt)
    den = lr * lr + li * li
    num_re = abar_re - 1.0
    coef_re = (num_re * lr + abar_im * li) / den
    coef_im = (abar_im * lr - num_re * li) / den
    br = b_re.astype(F32)
    bi = b_im.astype(F32)
    bbar_re = coef_re[..., None] * br - coef_im[..., None] * bi
    bbar_im = coef_re[..., None] * bi + coef_im[..., None] * br
    cr = c_re.astype(F32)
    ci = c_im.astype(F32)
    y = uf * d_skip.astype(F32).reshape(S5_GROUPS, S5_CH)
    for dirn in range(2):
        bu_re = jnp.einsum('blgh,gph->blgp', uf, bbar_re[dirn])
        bu_im = jnp.einsum('blgh,gph->blgp', uf, bbar_im[dirn])
        a_re = jnp.broadcast_to(abar_re[dirn], bu_re.shape)
        a_im = jnp.broadcast_to(abar_im[dirn], bu_re.shape)
        _, _, s_re, s_im = lax.associative_scan(
            _complex_linear_combine, (a_re, a_im, bu_re, bu_im), reverse=(dirn == 1), axis=1)
        y = y + jnp.einsum('blgp,ghp->blgh', s_re, cr[dirn]) - jnp.einsum('blgp,ghp->blgh', s_im, ci[dirn])
    y = y.reshape(bsz, L, GROUP_WIDTH)
    g = jax.nn.gelu(y)
    out = g * jax.nn.sigmoid(g @ w_glu.astype(F32))
    return out.astype(u.dtype)


def _linear_combine(e1, e2):
    a1, b1 = e1
    a2, b2 = e2
    return a1 * a2, a2 * b1 + b2


def rglru_mixer(z, conv_w, conv_b, w_gates, b_gates, lam):
    xr, xg = jnp.split(z, 2, axis=-1)
    xc = dwconv_centred(xr, conv_w, conv_b).astype(F32)
    bsz, L, W = xc.shape
    xb = xc.reshape(bsz, L, LRU_BLOCKS, LRU_BLOCK)
    pre = jnp.einsum('blnc,dgncm->dgblnm', xb, w_gates.astype(F32)).reshape(2, 2, bsz, L, W)
    gates = jax.nn.sigmoid(pre + b_gates.astype(F32)[:, :, None, None, :])
    r, i = gates[:, 0], gates[:, 1]
    log_a = LRU_C * r * jax.nn.log_sigmoid(lam.astype(F32))[:, None, None, :]
    a = jnp.exp(log_a)
    bx = jnp.sqrt(-jnp.expm1(2.0 * log_a)) * (i * xc[None])
    _, h_fwd = lax.associative_scan(_linear_combine, (a[0], bx[0]), axis=1)
    _, h_bwd = lax.associative_scan(_linear_combine, (a[1], bx[1]), axis=1, reverse=True)
    h = h_fwd + h_bwd
    return (h * jax.nn.gelu(xg.astype(F32))).astype(z.dtype)


def mla_mixer(z, q_norm_g, w_uq, kv_norm_g, w_ukv):
    bsz, L, _ = z.shape
    cq, ckv, kr = jnp.split(z, [MLA_Q_RANK, MLA_Q_RANK + MLA_KV_RANK], axis=-1)
    q = (rmsnorm(cq, q_norm_g) @ w_uq).reshape(bsz, L, MLA_HEADS, MLA_NOPE + MLA_ROPE).transpose(0, 2, 1, 3)
    kv = (rmsnorm(ckv, kv_norm_g) @ w_ukv).reshape(bsz, L, MLA_HEADS, MLA_NOPE + MLA_V).transpose(0, 2, 1, 3)
    pos = jnp.arange(L)
    q_nope = q[..., :MLA_NOPE]
    q_rope = rope(q[..., MLA_NOPE:], pos)
    k_nope = kv[..., :MLA_NOPE]
    v = kv[..., MLA_NOPE:]
    k_rope = rope(kr, pos)
    qb = min(Q_BLOCK, L)
    nb = L // qb
    qn_blocks = q_nope.reshape(bsz, MLA_HEADS, nb, qb, MLA_NOPE).transpose(2, 0, 1, 3, 4)
    qr_blocks = q_rope.reshape(bsz, MLA_HEADS, nb, qb, MLA_ROPE).transpose(2, 0, 1, 3, 4)
    scale = (MLA_NOPE + MLA_ROPE) ** -0.5

    def block(args):
        qn, qr = args
        s = (jnp.einsum('bhqd,bhkd->bhqk', qn, k_nope, preferred_element_type=F32)
             + jnp.einsum('bhqr,bkr->bhqk', qr, k_rope, preferred_element_type=F32)) * scale
        p = jax.nn.softmax(s, axis=-1)
        return jnp.einsum('bhqk,bhkv->bhqv', p.astype(v.dtype), v)

    o = lax.map(block, (qn_blocks, qr_blocks))
    return o.transpose(1, 0, 3, 2, 4).reshape(bsz, L, MLA_HEADS * MLA_V).astype(z.dtype)


def conv_ffn(h, w_up, conv_w, conv_b, w_down):
    up = dwconv_centred(h @ w_up, conv_w, conv_b)
    gate, val = jnp.split(up, 2, axis=-1)
    return (jax.nn.gelu(gate) * val) @ w_down


def _layer(x, l, p):
    lam_init = 0.8 - 0.6 * math.exp(-0.3 * l)
    h = rmsnorm(x, p['g_norms'][l, 0])
    z = h @ p['w_in'][l]
    za, zb, zc, zd = jnp.split(z, SPLITS, axis=-1)
    ya = diff_attention(za, p['diff_lambda'][l], p['diff_subln'][l], p['rel_bias'], lam_init)
    yb = s5_mixer(zb, p['s5_lambda_re'][l], p['s5_lambda_im'][l], p['s5_log_dt'][l],
                  p['s5_b_re'][l], p['s5_b_im'][l], p['s5_c_re'][l], p['s5_c_im'][l],
                  p['s5_d'][l], p['s5_w_glu'][l])
    yc = rglru_mixer(zc, p['lru_conv_w'][l], p['lru_conv_b'][l], p['lru_w_gates'][l],
                     p['lru_b_gates'][l], p['lru_lambda'][l])
    yd = mla_mixer(zd, p['mla_q_norm'][l], p['mla_w_uq'][l], p['mla_kv_norm'][l], p['mla_w_ukv'][l])
    yb = rmsnorm(yb, p['g_group'][l, 0])
    yc = rmsnorm(yc, p['g_group'][l, 1])
    yd = rmsnorm(yd, p['g_group'][l, 2])
    mix = jnp.concatenate([ya, yb.astype(ya.dtype), yc.astype(ya.dtype), yd.astype(ya.dtype)], axis=-1) @ p['w_out'][l]
    x = x + rmsnorm(mix, p['g_norms'][l, 1]).astype(x.dtype)
    f = conv_ffn(rmsnorm(x, p['g_norms'][l, 2]), p['w_up'][l], p['ffn_conv_w'][l],
                 p['ffn_conv_b'][l], p['w_down'][l])
    x = x + rmsnorm(f, p['g_norms'][l, 3]).astype(x.dtype)
    return x


def _trunk(x, p):
    for l in range(DEPTH):
        x = _layer(x, l, p)
    return x


def setup_inputs(seed: int = 0) -> dict:
    key = jax.random.key(seed)
    ks = jax.random.split(key, 32)
    nrm = jax.random.normal
    G, P, H = S5_GROUPS, S5_STATE, S5_CH
    u_lru = jax.random.uniform(ks[20], (DEPTH, 2, GROUP_WIDTH), minval=0.9, maxval=0.999)
    s_lru = u_lru ** (1.0 / LRU_C)
    return {
        'x_prompt': nrm(ks[0], (BATCH, SEQ, D_MODEL), F32),
        'x_sample': nrm(ks[1], (DEC_BATCH, DEC_SEQ, D_MODEL), F32),
        'g_norms': 1.0 + 0.02 * nrm(ks[2], (DEPTH, 4, D_MODEL), F32),
        'w_in': nrm(ks[3], (DEPTH, D_MODEL, N_IN), F32) * D_MODEL ** -0.5,
        'diff_lambda': 0.1 * nrm(ks[4], (DEPTH, 4, DIFF_HEAD_DIM), F32),
        'diff_subln': 1.0 + 0.02 * nrm(ks[5], (DEPTH, 2 * DIFF_HEAD_DIM), F32),
        'rel_bias': 0.5 * nrm(ks[6], (REL_BUCKETS, DIFF_HEADS), F32),
        's5_lambda_re': -0.5 + 0.01 * nrm(ks[7], (DEPTH, 2, G, P), F32),
        's5_lambda_im': math.pi * jnp.arange(P, dtype=F32) + 0.01 * nrm(ks[8], (DEPTH, 2, G, P), F32),
        's5_log_dt': jax.random.uniform(ks[9], (DEPTH, 2, G), minval=math.log(S5_DT_MIN), maxval=math.log(S5_DT_MAX)),
        's5_b_re': nrm(ks[10], (DEPTH, 2, G, P, H), F32) * (2.0 * H) ** -0.5,
        's5_b_im': nrm(ks[11], (DEPTH, 2, G, P, H), F32) * (2.0 * H) ** -0.5,
        's5_c_re': nrm(ks[12], (DEPTH, 2, G, H, P), F32) * (2.0 * P) ** -0.5,
        's5_c_im': nrm(ks[13], (DEPTH, 2, G, H, P), F32) * (2.0 * P) ** -0.5,
        's5_d': nrm(ks[14], (DEPTH, GROUP_WIDTH), F32),
        's5_w_glu': nrm(ks[15], (DEPTH, GROUP_WIDTH, GROUP_WIDTH), F32) * GROUP_WIDTH ** -0.5,
        'lru_conv_w': nrm(ks[16], (DEPTH, LRU_CONV, GROUP_WIDTH), F32) * LRU_CONV ** -0.5,
        'lru_conv_b': 0.01 * nrm(ks[17], (DEPTH, GROUP_WIDTH), F32),
        'lru_w_gates': nrm(ks[18], (DEPTH, 2, 2, LRU_BLOCKS, LRU_BLOCK, LRU_BLOCK), F32) * LRU_BLOCK ** -0.5,
        'lru_b_gates': 0.01 * nrm(ks[19], (DEPTH, 2, 2, GROUP_WIDTH), F32),
        'lru_lambda': jnp.log(s_lru) - jnp.log1p(-s_lru),
        'mla_q_norm': 1.0 + 0.02 * nrm(ks[21], (DEPTH, MLA_Q_RANK), F32),
        'mla_w_uq': nrm(ks[22], (DEPTH, MLA_Q_RANK, MLA_HEADS * (MLA_NOPE + MLA_ROPE)), F32) * MLA_Q_RANK ** -0.5,
        'mla_kv_norm': 1.0 + 0.02 * nrm(ks[23], (DEPTH, MLA_KV_RANK), F32),
        'mla_w_ukv': nrm(ks[24], (DEPTH, MLA_KV_RANK, MLA_HEADS * (MLA_NOPE + MLA_V)), F32) * MLA_KV_RANK ** -0.5,
        'g_group': 1.0 + 0.02 * nrm(ks[25], (DEPTH, 3, GROUP_WIDTH), F32),
        'w_out': nrm(ks[26], (DEPTH, MIX_WIDTH, D_MODEL), F32) * MIX_WIDTH ** -0.5,
        'w_up': nrm(ks[27], (DEPTH, D_MODEL, 2 * D_FF), F32) * D_MODEL ** -0.5,
        'ffn_conv_w': nrm(ks[28], (DEPTH, FFN_CONV, 2 * D_FF), F32) * FFN_CONV ** -0.5,
        'ffn_conv_b': 0.01 * nrm(ks[29], (DEPTH, 2 * D_FF), F32),
        'w_down': nrm(ks[30], (DEPTH, D_FF, D_MODEL), F32) * D_FF ** -0.5,
    }


def reference(x_prompt, x_sample, g_norms, w_in, diff_lambda, diff_subln, rel_bias,
              s5_lambda_re, s5_lambda_im, s5_log_dt, s5_b_re, s5_b_im, s5_c_re, s5_c_im,
              s5_d, s5_w_glu, lru_conv_w, lru_conv_b, lru_w_gates, lru_b_gates, lru_lambda,
              mla_q_norm, mla_w_uq, mla_kv_norm, mla_w_ukv, g_group, w_out, w_up,
              ffn_conv_w, ffn_conv_b, w_down):
    p = {
        'g_norms': g_norms, 'w_in': w_in, 'diff_lambda': diff_lambda, 'diff_subln': diff_subln,
        'rel_bias': rel_bias, 's5_lambda_re': s5_lambda_re, 's5_lambda_im': s5_lambda_im,
        's5_log_dt': s5_log_dt, 's5_b_re': s5_b_re, 's5_b_im': s5_b_im, 's5_c_re': s5_c_re,
        's5_c_im': s5_c_im, 's5_d': s5_d, 's5_w_glu': s5_w_glu, 'lru_conv_w': lru_conv_w,
        'lru_conv_b': lru_conv_b, 'lru_w_gates': lru_w_gates, 'lru_b_gates': lru_b_gates,
        'lru_lambda': lru_lambda, 'mla_q_norm': mla_q_norm, 'mla_w_uq': mla_w_uq,
        'mla_kv_norm': mla_kv_norm, 'mla_w_ukv': mla_w_ukv, 'g_group': g_group, 'w_out': w_out,
        'w_up': w_up, 'ffn_conv_w': ffn_conv_w, 'ffn_conv_b': ffn_conv_b, 'w_down': w_down,
    }
    y_prompt = _trunk(x_prompt, p)
    y_sample = _trunk(x_sample, p)
    return (y_prompt, y_sample)
```

```python
import functools
import math

import jax
import jax.numpy as jnp
from jax import lax
from jax.experimental import pallas as pl
from jax.experimental.pallas import tpu as pltpu

F32 = jnp.float32
BF16 = jnp.bfloat16

EPS = 1e-6
DIFF_HEAD_DIM = 128
REL_BUCKETS = 32
REL_MAX_DIST = 128
S5_CH = 16
LRU_C = 8.0
MLA_NOPE = 128
MLA_ROPE = 64
MLA_V = 128
ROPE_BASE = 10000.0
LANES = 128
SUBLANES = 8
VMEM_LIMIT = 56 * 1024 * 1024
ATTN_TILE = 512
SCAN_CHUNK = 512
FFN_ROWS = 512
FFN_COLS = 1024


def _params(sem):
    return pltpu.CompilerParams(dimension_semantics=sem, vmem_limit_bytes=VMEM_LIMIT)


def _pick(n, pref):
    t = min(n, pref)
    while n % t:
        t -= 8
    return t


def _gelu(x):
    return 0.5 * x * (1.0 + jnp.tanh(math.sqrt(2.0 / math.pi) * (x + 0.044715 * (x * x * x))))


def _sigmoid(x):
    return 1.0 / (1.0 + jnp.exp(-x))


def _rms_cast_kernel(x_ref, g_ref, o_ref):
    x = x_ref[...].astype(F32)
    ms = jnp.mean(x * x, axis=-1, keepdims=True)
    o_ref[...] = (x * lax.rsqrt(ms + EPS) * g_ref[...]).astype(o_ref.dtype)


def rms_cast(x, g, out_dtype=BF16, tm=256):
    m, d = x.shape
    tm = _pick(m, tm)
    return pl.pallas_call(
        _rms_cast_kernel,
        grid=(m // tm,),
        in_specs=[pl.BlockSpec((tm, d), lambda i: (i, 0)),
                  pl.BlockSpec((1, d), lambda i: (0, 0))],
        out_specs=pl.BlockSpec((tm, d), lambda i: (i, 0)),
        out_shape=jax.ShapeDtypeStruct((m, d), out_dtype),
        compiler_params=_params(("parallel",)),
        name="rms_cast",
    )(x, g.reshape(1, d))


def _resid_kernel(x_ref, y_ref, g_ref, gn_ref, xo_ref, ho_ref):
    y = y_ref[...]
    ms = jnp.mean(y * y, axis=-1, keepdims=True)
    xn = x_ref[...] + y * lax.rsqrt(ms + EPS) * g_ref[...]
    xo_ref[...] = xn
    ms2 = jnp.mean(xn * xn, axis=-1, keepdims=True)
    ho_ref[...] = (xn * lax.rsqrt(ms2 + EPS) * gn_ref[...]).astype(ho_ref.dtype)


def resid_norm(x, y, g, g_next, tm=256):
    m, d = x.shape
    tm = _pick(m, tm)
    row = pl.BlockSpec((tm, d), lambda i: (i, 0))
    vec = pl.BlockSpec((1, d), lambda i: (0, 0))
    return pl.pallas_call(
        _resid_kernel,
        grid=(m // tm,),
        in_specs=[row, row, vec, vec],
        out_specs=[row, row],
        out_shape=[jax.ShapeDtypeStruct((m, d), F32), jax.ShapeDtypeStruct((m, d), BF16)],
        compiler_params=_params(("parallel",)),
        name="resid_norm",
    )(x, y, g.reshape(1, d), g_next.reshape(1, d))


def _mm_kernel(a_ref, b_ref, o_ref):
    o_ref[...] = jnp.dot(a_ref[...], b_ref[...], preferred_element_type=F32).astype(o_ref.dtype)


def _mm_acc_kernel(a_ref, b_ref, o_ref, acc_ref):
    k = pl.program_id(2)

    @pl.when(k == 0)
    def _():
        acc_ref[...] = jnp.zeros_like(acc_ref)

    acc_ref[...] += jnp.dot(a_ref[...], b_ref[...], preferred_element_type=F32)

    @pl.when(k == pl.num_programs(2) - 1)
    def _():
        o_ref[...] = acc_ref[...].astype(o_ref.dtype)


def matmul(a, b, out_dtype, tm=1024, tn=1024, tk=4096):
    m, kd = a.shape
    n = b.shape[1]
    tm, tn = _pick(m, tm), _pick(n, tn)
    out_shape = jax.ShapeDtypeStruct((m, n), out_dtype)
    if kd <= tk:
        return pl.pallas_call(
            _mm_kernel,
            grid=(m // tm, n // tn),
            in_specs=[pl.BlockSpec((tm, kd), lambda i, j: (i, 0)),
                      pl.BlockSpec((kd, tn), lambda i, j: (0, j))],
            out_specs=pl.BlockSpec((tm, tn), lambda i, j: (i, j)),
            out_shape=out_shape,
            compiler_params=_params(("parallel", "parallel")),
            name="matmul",
        )(a, b)
    tk = _pick(kd, tk)
    return pl.pallas_call(
        _mm_acc_kernel,
        grid=(m // tm, n // tn, kd // tk),
        in_specs=[pl.BlockSpec((tm, tk), lambda i, j, k: (i, k)),
                  pl.BlockSpec((tk, tn), lambda i, j, k: (k, j))],
        out_specs=pl.BlockSpec((tm, tn), lambda i, j, k: (i, j)),
        out_shape=out_shape,
        scratch_shapes=[pltpu.VMEM((tm, tn), F32)],
        compiler_params=_params(("parallel", "parallel", "arbitrary")),
        name="matmul_acc",
    )(a, b)


def _softmax_step(s, v, m_ref, l_ref, acc_ref):
    m_prev = m_ref[...]
    m_new = jnp.maximum(m_prev, jnp.max(s, axis=-1, keepdims=True))
    alpha = jnp.exp(m_prev - m_new)
    p = jnp.exp(s - m_new)
    l_ref[...] = alpha * l_ref[...] + jnp.sum(p, axis=-1, keepdims=True)
    acc_ref[...] = alpha * acc_ref[...] + jnp.dot(p.astype(v.dtype), v, preferred_element_type=F32)
    m_ref[...] = m_new


def _qk(q, k):
    return lax.dot_general(q, k, (((1,), (1,)), ((), ())), preferred_element_type=F32)


def _diff_attn_kernel(far_ref, lam_ref, q1_ref, q2_ref, k1_ref, k2_ref, v_ref, bias_ref, g_ref, o_ref,
                      m1, l1, a1, m2, l2, a2, *, scale, out_scale):
    h = pl.program_id(1)
    qi = pl.program_id(2)
    ki = pl.program_id(3)

    @pl.when(ki == 0)
    def _():
        for m_ref, l_ref, a_ref in ((m1, l1, a1), (m2, l2, a2)):
            m_ref[...] = jnp.full_like(m_ref, -jnp.inf)
            l_ref[...] = jnp.zeros_like(l_ref)
            a_ref[...] = jnp.zeros_like(a_ref)

    def update(bias):
        v = v_ref[...]
        s1 = _qk(q1_ref[...], k1_ref[...]) * scale + bias
        _softmax_step(s1, v, m1, l1, a1)
        s2 = _qk(q2_ref[...], k2_ref[...]) * scale + bias
        _softmax_step(s2, v, m2, l2, a2)

    d = ki - qi
    near = jnp.abs(d) <= 1

    @pl.when(near)
    def _():
        update(bias_ref[jnp.clip(d, -1, 1) + 1])

    @pl.when(jnp.logical_not(near))
    def _():
        update(far_ref[h, (d > 0).astype(jnp.int32)])

    @pl.when(ki == pl.num_programs(3) - 1)
    def _():
        lam = lam_ref[0]
        o = a1[...] / l1[...] - lam * (a2[...] / l2[...])
        ms = jnp.mean(o * o, axis=-1, keepdims=True)
        o_ref[...] = (o * lax.rsqrt(ms + EPS) * g_ref[...] * out_scale).astype(o_ref.dtype)


def _rel_bucket(rel):
    half = REL_BUCKETS // 2
    exact = half // 2
    n = jnp.abs(rel)
    sign = jnp.where(rel > 0, half, 0)
    nf = jnp.maximum(n, exact).astype(F32)
    large = exact + (jnp.log(nf / exact) / math.log(REL_MAX_DIST / exact) * (half - exact)).astype(jnp.int32)
    large = jnp.minimum(large, half - 1)
    return sign + jnp.where(n < exact, n, large)


def diff_attention(za, lam, subln_g, rel_table, lam_init):
    bsz, L, w3 = za.shape
    gw = w3 // 3
    heads = gw // (2 * DIFF_HEAD_DIM)
    t = _pick(L, ATTN_TILE)
    assert t >= REL_MAX_DIST or t == L
    nb = L // t
    d = DIFF_HEAD_DIM
    kblk = gw // d
    vblk = 2 * gw // (2 * d)
    i = jnp.arange(t)
    rel = (jnp.arange(-1, 2) * t)[:, None, None] + i[None, None, :] - i[None, :, None]
    bias = rel_table[_rel_bucket(rel)].astype(F32).transpose(3, 0, 1, 2)
    big = jnp.array([-(REL_MAX_DIST + 1), REL_MAX_DIST + 1])
    far = rel_table[_rel_bucket(big)].astype(F32).T
    kernel = functools.partial(_diff_attn_kernel, scale=d ** -0.5, out_scale=1.0 - lam_init)
    smem = pl.BlockSpec(memory_space=pltpu.SMEM)
    return pl.pallas_call(
        kernel,
        grid=(bsz, heads, nb, nb),
        in_specs=[
            smem, smem,
            pl.BlockSpec((None, t, d), lambda b, h, qi, ki: (b, qi, 2 * h)),
            pl.BlockSpec((None, t, d), lambda b, h, qi, ki: (b, qi, 2 * h + 1)),
            pl.BlockSpec((None, t, d), lambda b, h, qi, ki: (b, ki, kblk + 2 * h)),
            pl.BlockSpec((None, t, d), lambda b, h, qi, ki: (b, ki, kblk + 2 * h + 1)),
            pl.BlockSpec((None, t, 2 * d), lambda b, h, qi, ki: (b, ki, vblk + h)),
            pl.BlockSpec((None, 3, t, t), lambda b, h, qi, ki: (h, 0, 0, 0)),
            pl.BlockSpec((1, 2 * d), lambda b, h, qi, ki: (0, 0)),
        ],
        out_specs=pl.BlockSpec((None, t, 2 * d), lambda b, h, qi, ki: (b, qi, h)),
        out_shape=jax.ShapeDtypeStruct((bsz, L, gw), BF16),
        scratch_shapes=[pltpu.VMEM((t, 1), F32), pltpu.VMEM((t, 1), F32), pltpu.VMEM((t, 2 * d), F32),
                        pltpu.VMEM((t, 1), F32), pltpu.VMEM((t, 1), F32), pltpu.VMEM((t, 2 * d), F32)],
        compiler_params=_params(("parallel", "parallel", "parallel", "arbitrary")),
        name="diff_attention",
    )(far, lam.reshape(1), za, za, za, za, za, bias, subln_g.reshape(1, 2 * d))


def _mla_attn_kernel(q_ref, k_ref, v_ref, o_ref, m_ref, l_ref, acc_ref, *, scale):
    ki = pl.program_id(3)

    @pl.when(ki == 0)
    def _():
        m_ref[...] = jnp.full_like(m_ref, -jnp.inf)
        l_ref[...] = jnp.zeros_like(l_ref)
        acc_ref[...] = jnp.zeros_like(acc_ref)

    s = _qk(q_ref[...], k_ref[...]) * scale
    _softmax_step(s, v_ref[...], m_ref, l_ref, acc_ref)

    @pl.when(ki == pl.num_programs(3) - 1)
    def _():
        o_ref[...] = (acc_ref[...] / l_ref[...]).astype(o_ref.dtype)


def mla_attention(q_cat, k_cat, kv):
    bsz, heads, L, dk = q_cat.shape
    t = _pick(L, ATTN_TILE)
    nb = L // t
    kernel = functools.partial(_mla_attn_kernel, scale=(MLA_NOPE + MLA_ROPE) ** -0.5)
    return pl.pallas_call(
        kernel,
        grid=(bsz, heads, nb, nb),
        in_specs=[
            pl.BlockSpec((None, None, t, dk), lambda b, h, qi, ki: (b, h, qi, 0)),
            pl.BlockSpec((None, None, t, dk), lambda b, h, qi, ki: (b, h, ki, 0)),
            pl.BlockSpec((None, t, MLA_V), lambda b, h, qi, ki: (b, ki, 2 * h + 1)),
        ],
        out_specs=pl.BlockSpec((None, t, MLA_V), lambda b, h, qi, ki: (b, qi, h)),
        out_shape=jax.ShapeDtypeStruct((bsz, L, heads * MLA_V), F32),
        scratch_shapes=[pltpu.VMEM((t, 1), F32), pltpu.VMEM((t, 1), F32), pltpu.VMEM((t, MLA_V), F32)],
        compiler_params=_params(("parallel", "parallel", "parallel", "arbitrary")),
        name="mla_attention",
    )(q_cat, k_cat, kv)


def _mla_prep_kernel(qn_ref, qr_ref, qs_ref, kn_ref, ka_ref, kb_ref, cos_ref, sin_ref, qo_ref, ko_ref):
    h = pl.program_id(2)
    cos = cos_ref[...]
    sin = sin_ref[...]
    qrot = qr_ref[...] * cos + qs_ref[...] * sin
    lane = lax.broadcasted_iota(jnp.int32, qrot.shape, 1)
    lo = (h % 2) * MLA_ROPE
    mine = jnp.logical_and(lane >= lo, lane < lo + MLA_ROPE)
    qo_ref[:, :MLA_NOPE] = qn_ref[...].astype(qo_ref.dtype)
    qo_ref[:, MLA_NOPE:] = jnp.where(mine, qrot, 0.0).astype(qo_ref.dtype)
    ko_ref[:, :MLA_NOPE] = kn_ref[...]
    ko_ref[:, MLA_NOPE:] = (ka_ref[...] * cos + kb_ref[...] * sin).astype(ko_ref.dtype)


def mla_prep(qproj, kv, krr, cos2, sin2, heads, tm=512):
    bsz, L, _ = qproj.shape
    tm = _pick(L, tm)
    nq = heads * MLA_NOPE // LANES
    npair = heads * MLA_ROPE // LANES
    out = jax.ShapeDtypeStruct((bsz, heads, L, 2 * LANES), BF16)
    blk = lambda f: pl.BlockSpec((None, tm, LANES), f)
    return pl.pallas_call(
        _mla_prep_kernel,
        grid=(bsz, L // tm, heads),
        in_specs=[
            blk(lambda b, i, h: (b, i, h)),
            blk(lambda b, i, h: (b, i, nq + h // 2)),
            blk(lambda b, i, h: (b, i, nq + npair + h // 2)),
            blk(lambda b, i, h: (b, i, 2 * h)),
            blk(lambda b, i, h: (b, i, 0)),
            blk(lambda b, i, h: (b, i, 1)),
            pl.BlockSpec((tm, LANES), lambda b, i, h: (i, 0)),
            pl.BlockSpec((tm, LANES), lambda b, i, h: (i, 0)),
        ],
        out_specs=[pl.BlockSpec((None, None, tm, 2 * LANES), lambda b, i, h: (b, h, i, 0))] * 2,
        out_shape=[out, out],
        compiler_params=_params(("parallel", "parallel", "arbitrary")),
        name="mla_prep",
    )(qproj, qproj, qproj, kv, krr, krr, cos2, sin2)


def _s5_dir_kernel(u_ref, wb_ref, wc_ref, cst_ref, y_ref, st_ref, carry_ref, *, reverse, nstate):
    c = pl.program_id(2)
    tc = st_ref.shape[0]
    ngrp = tc // SUBLANES

    @pl.when(c == 0)
    def _():
        carry_ref[...] = jnp.zeros_like(carry_ref)

    st_ref[...] = jnp.dot(u_ref[...].astype(BF16), wb_ref[...], preferred_element_type=F32)

    for lt in range(nstate // LANES):
        re_sl = slice(lt * LANES, (lt + 1) * LANES)
        im_sl = slice(nstate + lt * LANES, nstate + (lt + 1) * LANES)
        pw = [(cst_ref[j, :, re_sl], cst_ref[j, :, im_sl]) for j in range(4)]

        def body(g, carry, re_sl=re_sl, im_sl=im_sl, pw=pw):
            cr, ci = carry
            r = (ngrp - 1 - g) if reverse else g
            rows = pl.ds(pl.multiple_of(r * SUBLANES, SUBLANES), SUBLANES)
            xr = st_ref[rows, re_sl]
            xi = st_ref[rows, im_sl]
            for j, k in enumerate((1, 2, 4)):
                ar, ai = pw[j]
                sh = (SUBLANES - k) if reverse else k
                sr = pltpu.roll(xr, sh, 0)
                si = pltpu.roll(xi, sh, 0)
                xr, xi = xr + ar * sr - ai * si, xi + ar * si + ai * sr
            pr, pi = pw[3]
            xr, xi = xr + pr * cr - pi * ci, xi + pr * ci + pi * cr
            st_ref[rows, re_sl] = xr
            st_ref[rows, im_sl] = xi
            edge = 0 if reverse else SUBLANES - 1
            cr = jnp.broadcast_to(xr[edge:edge + 1, :], xr.shape)
            ci = jnp.broadcast_to(xi[edge:edge + 1, :], xi.shape)
            return cr, ci

        carry0 = (carry_ref[:, re_sl], carry_ref[:, im_sl])
        cr, ci = lax.fori_loop(0, ngrp, body, carry0, unroll=4)
        carry_ref[:, re_sl] = cr
        carry_ref[:, im_sl] = ci

    y_ref[...] = jnp.dot(st_ref[...].astype(BF16), wc_ref[...], preferred_element_type=F32)


def _s5_weights(lam_re, lam_im, log_dt, b_re, b_im, c_re, c_im):
    ndir, G, P = lam_re.shape
    H = b_re.shape[-1]
    gps = LANES // H
    nslab = G // gps
    nstate = gps * P
    dt = jnp.exp(log_dt.astype(F32))[..., None]
    lr, li = lam_re.astype(F32), lam_im.astype(F32)
    mag = jnp.exp(lr * dt)
    abar_re, abar_im = mag * jnp.cos(li * dt), mag * jnp.sin(li * dt)
    den = lr * lr + li * li
    num_re = abar_re - 1.0
    coef_re = (num_re * lr + abar_im * li) / den
    coef_im = (abar_im * lr - num_re * li) / den
    br, bi = b_re.astype(F32), b_im.astype(F32)
    bbar_re = coef_re[..., None] * br - coef_im[..., None] * bi
    bbar_im = coef_re[..., None] * bi + coef_im[..., None] * br
    eye = jnp.eye(gps, dtype=F32)

    def blockdiag_in(w):
        w = w.reshape(ndir, nslab, gps, P, H)
        return jnp.einsum('dsgph,gk->dsghkp', w, eye).reshape(ndir, nslab, gps * H, gps * P)

    def blockdiag_out(w):
        w = w.reshape(ndir, nslab, gps, H, P)
        return jnp.einsum('dsghp,gk->dsgpkh', w, eye).reshape(ndir, nslab, gps * P, gps * H)

    wb = jnp.concatenate([blockdiag_in(bbar_re), blockdiag_in(bbar_im)], axis=-1).astype(BF16)
    wc = jnp.concatenate([blockdiag_out(c_re.astype(F32)), -blockdiag_out(c_im.astype(F32))], axis=-2).astype(BF16)
    pr, pi = [abar_re], [abar_im]
    for _ in range(SUBLANES - 1):
        pr, pi = pr + [pr[-1] * abar_re - pi[-1] * abar_im], pi + [pr[-1] * abar_im + pi[-1] * abar_re]
    pr = jnp.stack(pr).reshape(SUBLANES, ndir, nslab, nstate)
    pi = jnp.stack(pi).reshape(SUBLANES, ndir, nslab, nstate)
    row = jnp.arange(SUBLANES)
    consts = []
    for dirn in range(ndir):
        tabs = []
        for k in (1, 2, 4):
            keep = (row >= k) if dirn == 0 else (row < SUBLANES - k)
            tabs.append(jnp.concatenate([jnp.where(keep[:, None, None], pr[k - 1, dirn][None], 0.0),
                                         jnp.where(keep[:, None, None], pi[k - 1, dirn][None], 0.0)], axis=-1))
        order = row if dirn == 0 else row[::-1]
        tabs.append(jnp.concatenate([pr[order, dirn], pi[order, dirn]], axis=-1))
        consts.append(jnp.stack(tabs))
    cst = jnp.stack(consts).transpose(0, 3, 1, 2, 4)
    return wb, wc, cst, nstate


def s5_scan(u, wb, wc, cst, nstate, reverse):
    bsz, L, gw = u.shape
    tc = _pick(L, SCAN_CHUNK)
    nc = L // tc
    nslab = gw // LANES
    cidx = (lambda c: nc - 1 - c) if reverse else (lambda c: c)
    kernel = functools.partial(_s5_dir_kernel, reverse=reverse, nstate=nstate)
    return pl.pallas_call(
        kernel,
        grid=(bsz, nslab, nc),
        in_specs=[
            pl.BlockSpec((None, tc, LANES), lambda b, s, c: (b, cidx(c), s)),
            pl.BlockSpec((None, LANES, 2 * nstate), lambda b, s, c: (s, 0, 0)),
            pl.BlockSpec((None, 2 * nstate, LANES), lambda b, s, c: (s, 0, 0)),
            pl.BlockSpec((None, 4, SUBLANES, 2 * nstate), lambda b, s, c: (s, 0, 0, 0)),
        ],
        out_specs=pl.BlockSpec((None, tc, LANES), lambda b, s, c: (b, cidx(c), s)),
        out_shape=jax.ShapeDtypeStruct((bsz, L, gw), F32),
        scratch_shapes=[pltpu.VMEM((tc, 2 * nstate), F32), pltpu.VMEM((SUBLANES, 2 * nstate), F32)],
        compiler_params=_params(("parallel", "parallel", "arbitrary")),
        name="s5_scan_bwd" if reverse else "s5_scan_fwd",
    )(u, wb, wc, cst)


def _s5_post_kernel(u_ref, yf_ref, yb_ref, d_ref, w_ref, g_ref, o_ref):
    y = u_ref[...] * d_ref[...] + yf_ref[...] + yb_ref[...]
    g = _gelu(y)
    gate = jnp.dot(g.astype(BF16), w_ref[...], preferred_element_type=F32)
    out = g * _sigmoid(gate)
    ms = jnp.mean(out * out, axis=-1, keepdims=True)
    o_ref[...] = (out * lax.rsqrt(ms + EPS) * g_ref[...]).astype(o_ref.dtype)


def s5_post(u, yf, yb, d_skip, w_glu, g_group, tm=512):
    m, gw = u.shape
    tm = _pick(m, tm)
    row = pl.BlockSpec((tm, gw), lambda i: (i, 0))
    vec = pl.BlockSpec((1, gw), lambda i: (0, 0))
    return pl.pallas_call(
        _s5_post_kernel,
        grid=(m // tm,),
        in_specs=[row, row, row, vec, pl.BlockSpec((gw, gw), lambda i: (0, 0)), vec],
        out_specs=row,
        out_shape=jax.ShapeDtypeStruct((m, gw), BF16),
        compiler_params=_params(("parallel",)),
        name="s5_post",
    )(u, yf, yb, d_skip.reshape(1, gw), w_glu, g_group.reshape(1, gw))


def _shift_rows(x, halo, k, valid):
    sh = pltpu.roll(x, k, 0)
    hs = pltpu.roll(jnp.where(valid, halo, 0.0), k, 0)
    row = lax.broadcasted_iota(jnp.int32, hs.shape, 0)
    head = jnp.where(row < k, hs, sh[:SUBLANES])
    return jnp.concatenate([head, sh[SUBLANES:]], axis=0)


def _shift_rows_up(x, halo, k, valid):
    t = x.shape[0]
    sh = pltpu.roll(x, t - k, 0)
    hs = pltpu.roll(jnp.where(valid, halo, 0.0), SUBLANES - k, 0)
    row = lax.broadcasted_iota(jnp.int32, hs.shape, 0)
    tail = jnp.where(row >= SUBLANES - k, hs, sh[t - SUBLANES:])
    return jnp.concatenate([sh[:t - SUBLANES], tail], axis=0)


def _lru_dir_kernel(x_ref, xp_ref, xn_ref, cw_ref, cb_ref, w_ref, bg_ref, lam_ref, h_ref,
                    a_s, b_s, carry_ref, *, reverse, blk):
    c = pl.program_id(1)
    nc = pl.num_programs(1)
    tc, gw = x_ref.shape
    ngrp = tc // SUBLANES
    pos = (nc - 1 - c) if reverse else c

    @pl.when(c == 0)
    def _():
        carry_ref[...] = jnp.zeros_like(carry_ref)

    x = x_ref[...]
    has_prev = pos > 0
    has_next = pos < nc - 1
    xc = (_shift_rows(x, xp_ref[...], 2, has_prev) * cw_ref[0:1, :]
          + _shift_rows(x, xp_ref[...], 1, has_prev) * cw_ref[1:2, :]
          + x * cw_ref[2:3, :]
          + _shift_rows_up(x, xn_ref[...], 1, has_next) * cw_ref[3:4, :]) + cb_ref[...]
    xcb = xc.astype(BF16)
    lam = lam_ref[...]
    log_sig = jnp.minimum(lam, 0.0) - jnp.log(1.0 + jnp.exp(-jnp.abs(lam)))
    for n in range(gw // blk):
        cols = slice(n * blk, (n + 1) * blk)
        pre = jnp.dot(xcb[:, cols], w_ref[n], preferred_element_type=F32)
        r = _sigmoid(pre[:, :blk] + bg_ref[0:1, cols])
        i = _sigmoid(pre[:, blk:] + bg_ref[1:2, cols])
        a = jnp.exp(LRU_C * r * log_sig[:, cols])
        a_s[:, cols] = a
        b_s[:, cols] = jnp.sqrt(1.0 - a * a) * (i * xc[:, cols])

    row = lax.broadcasted_iota(jnp.int32, (SUBLANES, LANES), 0)
    for lt in range(gw // LANES):
        cols = slice(lt * LANES, (lt + 1) * LANES)

        def body(g, carry, cols=cols):
            r = (ngrp - 1 - g) if reverse else g
            rows = pl.ds(pl.multiple_of(r * SUBLANES, SUBLANES), SUBLANES)
            a = a_s[rows, cols]
            b = b_s[rows, cols]
            for k in (1, 2, 4):
                sh = (SUBLANES - k) if reverse else k
                keep = (row < SUBLANES - k) if reverse else (row >= k)
                b = b + a * jnp.where(keep, pltpu.roll(b, sh, 0), 0.0)
                a = a * jnp.where(keep, pltpu.roll(a, sh, 0), 1.0)
            hval = b + a * carry
            h_ref[rows, cols] = hval
            edge = 0 if reverse else SUBLANES - 1
            return jnp.broadcast_to(hval[edge:edge + 1, :], hval.shape)

        carry_ref[:, cols] = lax.fori_loop(0, ngrp, body, carry_ref[:, cols], unroll=4)


def lru_scan(zc, conv_w, conv_b, w_gates, b_gates, lam, reverse):
    bsz, L, w2 = zc.shape
    gw = w2 // 2
    nblk, blk = w_gates.shape[1], w_gates.shape[2]
    tc = _pick(L, SCAN_CHUNK)
    nc = L // tc
    hb = tc // SUBLANES
    cidx = (lambda c: nc - 1 - c) if reverse else (lambda c: c)
    wcat = jnp.concatenate([w_gates[0], w_gates[1]], axis=-1).astype(BF16)
    kernel = functools.partial(_lru_dir_kernel, reverse=reverse, blk=blk)
    vec = lambda r: pl.BlockSpec((r, gw), lambda b, c: (0, 0))
    return pl.pallas_call(
        kernel,
        grid=(bsz, nc),
        in_specs=[
            pl.BlockSpec((None, tc, gw), lambda b, c: (b, cidx(c), 0)),
            pl.BlockSpec((None, SUBLANES, gw), lambda b, c: (b, jnp.maximum(cidx(c) * hb - 1, 0), 0)),
            pl.BlockSpec((None, SUBLANES, gw), lambda b, c: (b, jnp.minimum((cidx(c) + 1) * hb, L // SUBLANES - 1), 0)),
            vec(4), vec(1),
            pl.BlockSpec((nblk, blk, 2 * blk), lambda b, c: (0, 0, 0)),
            vec(2), vec(1),
        ],
        out_specs=pl.BlockSpec((None, tc, gw), lambda b, c: (b, cidx(c), 0)),
        out_shape=jax.ShapeDtypeStruct((bsz, L, gw), F32),
        scratch_shapes=[pltpu.VMEM((tc, gw), F32), pltpu.VMEM((tc, gw), F32), pltpu.VMEM((SUBLANES, gw), F32)],
        compiler_params=_params(("parallel", "arbitrary")),
        name="lru_scan_bwd" if reverse else "lru_scan_fwd",
    )(zc, zc, zc, conv_w, conv_b.reshape(1, gw), wcat, b_gates, lam.reshape(1, gw))


def _lru_post_kernel(hf_ref, hb_ref, xg_ref, g_ref, o_ref):
    out = (hf_ref[...] + hb_ref[...]) * _gelu(xg_ref[...])
    ms = jnp.mean(out * out, axis=-1, keepdims=True)
    o_ref[...] = (out * lax.rsqrt(ms + EPS) * g_ref[...]).astype(o_ref.dtype)


def lru_post(hf, hb, zc, g_group, tm=512):
    m, gw = hf.shape
    tm = _pick(m, tm)
    row = pl.BlockSpec((tm, gw), lambda i: (i, 0))
    return pl.pallas_call(
        _lru_post_kernel,
        grid=(m // tm,),
        in_specs=[row, row, pl.BlockSpec((tm, gw), lambda i: (i, 1)), pl.BlockSpec((1, gw), lambda i: (0, 0))],
        out_specs=row,
        out_shape=jax.ShapeDtypeStruct((m, gw), BF16),
        compiler_params=_params(("parallel",)),
        name="lru_post",
    )(hf, hb, zc, g_group.reshape(1, gw))


def _ffn_act_kernel(g_ref, gp_ref, gn_ref, v_ref, vp_ref, vn_ref, wg_ref, wv_ref, bg_ref, bv_ref, o_ref):
    i = pl.program_id(1)
    has_prev = i > 0
    has_next = i < pl.num_programs(1) - 1

    def conv(x_ref, p_ref, n_ref, w_ref, b_ref):
        x = x_ref[...].astype(F32)
        prev = p_ref[...].astype(F32)[-SUBLANES:]
        nxt = n_ref[...].astype(F32)[:SUBLANES]
        return (_shift_rows(x, prev, 1, has_prev) * w_ref[0:1, :]
                + x * w_ref[1:2, :]
                + _shift_rows_up(x, nxt, 1, has_next) * w_ref[2:3, :]) + b_ref[...]

    gate = conv(g_ref, gp_ref, gn_ref, wg_ref, bg_ref)
    val = conv(v_ref, vp_ref, vn_ref, wv_ref, bv_ref)
    o_ref[...] = (_gelu(gate) * val).astype(o_ref.dtype)


def ffn_act(up, conv_w, conv_b):
    bsz, L, w2 = up.shape
    dff = w2 // 2
    tm, tc = _pick(L, FFN_ROWS), _pick(dff, FFN_COLS)
    hb = tm // 16 if up.dtype == BF16 else tm // SUBLANES
    hrows = tm // hb
    nj = dff // tc
    nh = L // hrows
    main = lambda off: pl.BlockSpec((None, tm, tc), lambda b, i, j: (b, i, off + j))
    prev = lambda off: pl.BlockSpec((None, hrows, tc), lambda b, i, j: (b, jnp.maximum(i * hb - 1, 0), off + j))
    nxt = lambda off: pl.BlockSpec((None, hrows, tc), lambda b, i, j: (b, jnp.minimum((i + 1) * hb, nh - 1), off + j))
    wspec = lambda off: pl.BlockSpec((3, tc), lambda b, i, j: (0, off + j))
    bspec = lambda off: pl.BlockSpec((1, tc), lambda b, i, j: (0, off + j))
    return pl.pallas_call(
        _ffn_act_kernel,
        grid=(bsz, L // tm, nj),
        in_specs=[main(0), prev(0), nxt(0), main(nj), prev(nj), nxt(nj),
                  wspec(0), wspec(nj), bspec(0), bspec(nj)],
        out_specs=pl.BlockSpec((None, tm, tc), lambda b, i, j: (b, i, j)),
        out_shape=jax.ShapeDtypeStruct((bsz, L, dff), BF16),
        compiler_params=_params(("parallel", "parallel", "parallel")),
        name="ffn_act",
    )(up, up, up, up, up, up, conv_w, conv_w, conv_b.reshape(1, w2), conv_b.reshape(1, w2))


def _prep_layer(l, p):
    d = p['w_in'].shape[1]
    gw = d // 4
    q_rank = p['mla_q_norm'].shape[1]
    kv_rank = p['mla_kv_norm'].shape[1]
    heads = gw // MLA_V
    o = [3 * gw, 4 * gw, 6 * gw, 6 * gw + q_rank, 6 * gw + q_rank + kv_rank]
    w_in = p['w_in'][l]
    half = MLA_ROPE // 2
    swap = jnp.concatenate([jnp.arange(half, MLA_ROPE), jnp.arange(half)])
    w_kr = w_in[:, o[4]:]
    w_kr_s = w_kr[:, swap]
    w_uq = p['mla_w_uq'][l].reshape(q_rank, heads, MLA_NOPE + MLA_ROPE)
    uq_nope = w_uq[:, :, :MLA_NOPE].reshape(q_rank, heads * MLA_NOPE)
    uq_rope = w_uq[:, :, MLA_NOPE:]
    uq_rope_s = uq_rope[:, :, swap]
    lf = p['diff_lambda'][l].astype(F32)
    lam_init = 0.8 - 0.6 * math.exp(-0.3 * l)
    wb, wc, cst, nstate = _s5_weights(p['s5_lambda_re'][l], p['s5_lambda_im'][l], p['s5_log_dt'][l],
                                      p['s5_b_re'][l], p['s5_b_im'][l], p['s5_c_re'][l], p['s5_c_im'][l])
    return dict(
        lam_init=lam_init,
        lam=jnp.exp(jnp.sum(lf[0] * lf[1])) - jnp.exp(jnp.sum(lf[2] * lf[3])) + lam_init,
        w_a=w_in[:, :o[0]].astype(BF16), w_b=w_in[:, o[0]:o[1]].astype(BF16),
        w_c=w_in[:, o[1]:o[2]].astype(BF16), w_cq=w_in[:, o[2]:o[3]].astype(BF16),
        w_ckv=w_in[:, o[3]:o[4]].astype(BF16),
        w_kr=jnp.concatenate([w_kr, w_kr, w_kr_s, w_kr_s], axis=1).astype(BF16),
        w_uq=jnp.concatenate([uq_nope, uq_rope.reshape(q_rank, -1), uq_rope_s.reshape(q_rank, -1)], axis=1).astype(BF16),
        w_ukv=p['mla_w_ukv'][l].astype(BF16),
        s5_wb=wb, s5_wc=wc, s5_cst=cst, s5_nstate=nstate,
        w_glu=p['s5_w_glu'][l].astype(BF16),
        w_out=p['w_out'][l].astype(BF16), w_up=p['w_up'][l].astype(BF16), w_down=p['w_down'][l].astype(BF16),
    )


def _rope_tables(L):
    half = MLA_ROPE // 2
    inv = ROPE_BASE ** (-jnp.arange(half, dtype=F32) / half)
    ang = jnp.arange(L).astype(F32)[:, None] * inv[None, :]
    cos, sin = jnp.cos(ang), jnp.sin(ang)
    reps = LANES // MLA_ROPE
    return jnp.tile(jnp.concatenate([cos, cos], axis=1), (1, reps)), jnp.tile(jnp.concatenate([-sin, sin], axis=1), (1, reps))


def _layer(x, h, l, p, w, g_next, bsz, L):
    m, d = x.shape
    gw = d // 4
    heads = gw // MLA_V
    b3 = lambda a: a.reshape(bsz, L, a.shape[-1])
    f2 = lambda a: a.reshape(m, a.shape[-1])
    za = matmul(h, w['w_a'], BF16)
    ya = f2(diff_attention(b3(za), w['lam'], p['diff_subln'][l], p['rel_bias'], w['lam_init']))
    zb = matmul(h, w['w_b'], F32)
    yf = s5_scan(b3(zb), w['s5_wb'][0], w['s5_wc'][0], w['s5_cst'][0], w['s5_nstate'], False)
    yr = s5_scan(b3(zb), w['s5_wb'][1], w['s5_wc'][1], w['s5_cst'][1], w['s5_nstate'], True)
    yb = s5_post(zb, f2(yf), f2(yr), p['s5_d'][l], w['w_glu'], p['g_group'][l, 0])
    zc = matmul(h, w['w_c'], F32)
    hf = lru_scan(b3(zc), p['lru_conv_w'][l], p['lru_conv_b'][l], p['lru_w_gates'][l, 0],
                  p['lru_b_gates'][l, 0], p['lru_lambda'][l, 0], False)
    hr = lru_scan(b3(zc), p['lru_conv_w'][l], p['lru_conv_b'][l], p['lru_w_gates'][l, 1],
                  p['lru_b_gates'][l, 1], p['lru_lambda'][l, 1], True)
    yc = lru_post(f2(hf), f2(hr), zc, p['g_group'][l, 1])
    cq = rms_cast(matmul(h, w['w_cq'], F32), p['mla_q_norm'][l])
    ckv = rms_cast(matmul(h, w['w_ckv'], F32), p['mla_kv_norm'][l])
    qproj = matmul(cq, w['w_uq'], F32)
    kv = matmul(ckv, w['w_ukv'], BF16)
    krr = matmul(h, w['w_kr'], F32)
    cos2, sin2 = _rope_tables(L)
    q_cat, k_cat = mla_prep(b3(qproj), b3(kv), b3(krr), cos2, sin2, heads)
    yd = rms_cast(f2(mla_attention(q_cat, k_cat, b3(kv))), p['g_group'][l, 2])
    mix = matmul(jnp.concatenate([ya, yb, yc, yd], axis=1), w['w_out'], F32)
    x, h2 = resid_norm(x, mix, p['g_norms'][l, 1], p['g_norms'][l, 2])
    up = matmul(h2, w['w_up'], BF16)
    act = ffn_act(b3(up), p['ffn_conv_w'][l], p['ffn_conv_b'][l])
    f = matmul(f2(act), w['w_down'], F32)
    return resid_norm(x, f, p['g_norms'][l, 3], g_next)


def _trunk(x3, p, ws):
    bsz, L, d = x3.shape
    depth = len(ws)
    x = x3.reshape(bsz * L, d)
    h = rms_cast(x, p['g_norms'][0, 0])
    for l in range(depth):
        g_next = p['g_norms'][l + 1, 0] if l + 1 < depth else p['g_norms'][l, 0]
        x, h = _layer(x, h, l, p, ws[l], g_next, bsz, L)
    return x.reshape(bsz, L, d)


def kernel(x_prompt, x_sample, g_norms, w_in, diff_lambda, diff_subln, rel_bias, s5_lambda_re, s5_lambda_im, s5_log_dt, s5_b_re, s5_b_im, s5_c_re, s5_c_im, s5_d, s5_w_glu, lru_conv_w, lru_conv_b, lru_w_gates, lru_b_gates, lru_lambda, mla_q_norm, mla_w_uq, mla_kv_norm, mla_w_ukv, g_group, w_out, w_up, ffn_conv_w, ffn_conv_b, w_down):
    p = {
        'g_norms': g_norms, 'w_in': w_in, 'diff_lambda': diff_lambda, 'diff_subln': diff_subln,
        'rel_bias': rel_bias, 's5_lambda_re': s5_lambda_re, 's5_lambda_im': s5_lambda_im,
        's5_log_dt': s5_log_dt, 's5_b_re': s5_b_re, 's5_b_im': s5_b_im, 's5_c_re': s5_c_re,
        's5_c_im': s5_c_im, 's5_d': s5_d, 's5_w_glu': s5_w_glu, 'lru_conv_w': lru_conv_w,
        'lru_conv_b': lru_conv_b, 'lru_w_gates': lru_w_gates, 'lru_b_gates': lru_b_gates,
        'lru_lambda': lru_lambda, 'mla_q_norm': mla_q_norm, 'mla_w_uq': mla_w_uq,
        'mla_kv_norm': mla_kv_norm, 'mla_w_ukv': mla_w_ukv, 'g_group': g_group, 'w_out': w_out,
        'w_up': w_up, 'ffn_conv_w': ffn_conv_w, 'ffn_conv_b': ffn_conv_b, 'w_down': w_down,
    }
    ws = [_prep_layer(l, p) for l in range(w_in.shape[0])]
    return (_trunk(x_prompt, p, ws), _trunk(x_sample, p, ws))
```

```python
import functools
import math

import jax
import jax.numpy as jnp
from jax import lax
from jax.experimental import pallas as pl
from jax.experimental.pallas import tpu as pltpu

F32 = jnp.float32
BF16 = jnp.bfloat16

EPS = 1e-6
DIFF_HEAD_DIM = 128
REL_BUCKETS = 32
REL_MAX_DIST = 128
S5_CH = 16
LRU_C = 8.0
MLA_NOPE = 128
MLA_ROPE = 64
MLA_V = 128
ROPE_BASE = 10000.0
LANES = 128
SUBLANES = 8
VMEM_LIMIT = 56 * 1024 * 1024
ATTN_TQ = 1024
ATTN_TK = 512
ATTN_STRIP = 256
SCAN_CHUNK = 512
FFN_ROWS = 512
FFN_COLS = 1024
FFN_SUB = 256


def _params(sem):
    return pltpu.CompilerParams(dimension_semantics=sem, vmem_limit_bytes=VMEM_LIMIT)


def _pick(n, pref):
    t = min(n, pref)
    while n % t:
        t -= 8
    return t


def _gelu(x):
    return 0.5 * x * (1.0 + jnp.tanh(math.sqrt(2.0 / math.pi) * (x + 0.044715 * (x * x * x))))


def _sigmoid(x):
    return 1.0 / (1.0 + jnp.exp(-x))


def _rms_cast_kernel(x_ref, g_ref, o_ref):
    x = x_ref[...].astype(F32)
    ms = jnp.mean(x * x, axis=-1, keepdims=True)
    o_ref[...] = (x * lax.rsqrt(ms + EPS) * g_ref[...]).astype(o_ref.dtype)


def rms_cast(x, g, out_dtype=BF16, tm=256):
    m, d = x.shape
    tm = _pick(m, tm)
    return pl.pallas_call(
        _rms_cast_kernel,
        grid=(m // tm,),
        in_specs=[pl.BlockSpec((tm, d), lambda i: (i, 0)),
                  pl.BlockSpec((1, d), lambda i: (0, 0))],
        out_specs=pl.BlockSpec((tm, d), lambda i: (i, 0)),
        out_shape=jax.ShapeDtypeStruct((m, d), out_dtype),
        compiler_params=_params(("parallel",)),
        name="rms_cast",
    )(x, g.reshape(1, d))


def _resid_kernel(x_ref, y_ref, g_ref, gn_ref, xo_ref, ho_ref):
    y = y_ref[...]
    ms = jnp.mean(y * y, axis=-1, keepdims=True)
    xn = x_ref[...] + y * lax.rsqrt(ms + EPS) * g_ref[...]
    xo_ref[...] = xn
    ms2 = jnp.mean(xn * xn, axis=-1, keepdims=True)
    ho_ref[...] = (xn * lax.rsqrt(ms2 + EPS) * gn_ref[...]).astype(ho_ref.dtype)


def resid_norm(x, y, g, g_next, tm=256):
    m, d = x.shape
    tm = _pick(m, tm)
    row = pl.BlockSpec((tm, d), lambda i: (i, 0))
    vec = pl.BlockSpec((1, d), lambda i: (0, 0))
    return pl.pallas_call(
        _resid_kernel,
        grid=(m // tm,),
        in_specs=[row, row, vec, vec],
        out_specs=[row, row],
        out_shape=[jax.ShapeDtypeStruct((m, d), F32), jax.ShapeDtypeStruct((m, d), BF16)],
        compiler_params=_params(("parallel",)),
        name="resid_norm",
    )(x, y, g.reshape(1, d), g_next.reshape(1, d))


def _mm_kernel(a_ref, b_ref, o_ref, *, scale):
    acc = jnp.dot(a_ref[...], b_ref[...], preferred_element_type=F32)
    o_ref[...] = (acc if scale is None else acc * scale).astype(o_ref.dtype)


def matmul(a, b, out_dtype, tm=1024, tn=1024, scale=None):
    m, kd = a.shape
    n = b.shape[1]
    tm, tn = _pick(m, tm), _pick(n, tn)
    return pl.pallas_call(
        functools.partial(_mm_kernel, scale=scale),
        grid=(m // tm, n // tn),
        in_specs=[pl.BlockSpec((tm, kd), lambda i, j: (i, 0)),
                  pl.BlockSpec((kd, tn), lambda i, j: (0, j))],
        out_specs=pl.BlockSpec((tm, tn), lambda i, j: (i, j)),
        out_shape=jax.ShapeDtypeStruct((m, n), out_dtype),
        compiler_params=_params(("parallel", "parallel")),
        name="matmul",
    )(a, b)


LOG2E = math.log2(math.e)


def _qk(q, k):
    return lax.dot_general(q, k, (((1,), (1,)), ((), ())), preferred_element_type=F32)


def _lanes(x, n):
    return x[:, :n] if n <= LANES else jnp.tile(x, (1, n // LANES))


def _flash_strip(q, k, v, bias, rows, m_ref, l_ref, acc_ref):
    s = _qk(q, k)
    if bias is not None:
        s = s + bias
    m_prev = m_ref[rows, :]
    m_new = jnp.maximum(m_prev, jnp.max(s, axis=-1, keepdims=True))
    p = jnp.exp2(s - _lanes(m_new, s.shape[1]))
    alpha = jnp.exp2(m_prev - m_new)
    l_ref[rows, :] = alpha * l_ref[rows, :] + jnp.sum(p, axis=-1, keepdims=True)
    acc_ref[rows, :] = (_lanes(alpha, acc_ref.shape[1]) * acc_ref[rows, :]
                        + jnp.dot(p.astype(v.dtype), v, preferred_element_type=F32))
    m_ref[rows, :] = m_new


def _flash_init(*refs):
    for m_ref, l_ref, a_ref in refs:
        m_ref[...] = jnp.full_like(m_ref, -jnp.inf)
        l_ref[...] = jnp.zeros_like(l_ref)
        a_ref[...] = jnp.zeros_like(a_ref)


def _diff_attn_kernel(lam_ref, q1_ref, q2_ref, k1_ref, k2_ref, v_ref, bias_ref, g_ref, o_ref,
                      m1, l1, a1, m2, l2, a2, *, strip, off_lo, out_scale):
    qi = pl.program_id(2)
    ki = pl.program_id(3)
    tq, tk = q1_ref.shape[0], k1_ref.shape[0]
    n_off = bias_ref.shape[0]

    @pl.when(ki == 0)
    def _():
        _flash_init((m1, l1, a1), (m2, l2, a2))

    v = v_ref[...]
    k1 = k1_ref[...]
    k2 = k2_ref[...]
    tile_off = ki * (tk // strip) - qi * (tq // strip)
    for r in range(tq // strip):
        rows = slice(r * strip, (r + 1) * strip)
        bias = bias_ref[jnp.clip(tile_off - r - off_lo, 0, n_off - 1)]
        _flash_strip(q1_ref[rows, :], k1, v, bias, rows, m1, l1, a1)
        _flash_strip(q2_ref[rows, :], k2, v, bias, rows, m2, l2, a2)

    @pl.when(ki == pl.num_programs(3) - 1)
    def _():
        lam = lam_ref[0]
        dv = a1.shape[1]
        o = a1[...] / _lanes(l1[...], dv) - lam * (a2[...] / _lanes(l2[...], dv))
        ms = jnp.mean(o * o, axis=-1, keepdims=True)
        o_ref[...] = (o * lax.rsqrt(ms + EPS) * g_ref[...] * out_scale).astype(o_ref.dtype)


def _rel_bucket(rel):
    half = REL_BUCKETS // 2
    exact = half // 2
    n = jnp.abs(rel)
    sign = jnp.where(rel > 0, half, 0)
    nf = jnp.maximum(n, exact).astype(F32)
    large = exact + (jnp.log(nf / exact) / math.log(REL_MAX_DIST / exact) * (half - exact)).astype(jnp.int32)
    large = jnp.minimum(large, half - 1)
    return sign + jnp.where(n < exact, n, large)


def _attn_tiles(L):
    tq, tk = _pick(L, ATTN_TQ), _pick(L, ATTN_TK)
    strip = _pick(math.gcd(tq, tk), ATTN_STRIP)
    return tq, tk, strip


def diff_attention(zq, zkv, lam, subln_g, rel_table, lam_init):
    bsz, L, gw = zq.shape
    d = DIFF_HEAD_DIM
    heads = gw // (2 * d)
    tq, tk, strip = _attn_tiles(L)
    vblk = gw // (2 * d)
    off_hi = (REL_MAX_DIST - 2 + strip) // strip + 1
    off_lo = -((REL_MAX_DIST + tk - 2) // strip) - 1
    offs = jnp.arange(off_lo, off_hi + 1) * strip
    rel = offs[:, None, None] + jnp.arange(tk)[None, None, :] - jnp.arange(strip)[None, :, None]
    onehot = _rel_bucket(rel)[None, ..., None] == jnp.arange(REL_BUCKETS)
    tab = rel_table.astype(F32).T[:, None, None, None, :]
    bias = jnp.sum(jnp.where(onehot, tab, 0.0), axis=-1) * LOG2E
    kernel = functools.partial(_diff_attn_kernel, strip=strip, off_lo=off_lo, out_scale=1.0 - lam_init)
    stat = lambda w: pltpu.VMEM((tq, w), F32)
    return pl.pallas_call(
        kernel,
        grid=(bsz, heads, L // tq, L // tk),
        in_specs=[
            pl.BlockSpec(memory_space=pltpu.SMEM),
            pl.BlockSpec((None, tq, d), lambda b, h, qi, ki: (b, qi, 2 * h)),
            pl.BlockSpec((None, tq, d), lambda b, h, qi, ki: (b, qi, 2 * h + 1)),
            pl.BlockSpec((None, tk, d), lambda b, h, qi, ki: (b, ki, 2 * h)),
            pl.BlockSpec((None, tk, d), lambda b, h, qi, ki: (b, ki, 2 * h + 1)),
            pl.BlockSpec((None, tk, 2 * d), lambda b, h, qi, ki: (b, ki, vblk + h)),
            pl.BlockSpec((None, off_hi - off_lo + 1, strip, tk), lambda b, h, qi, ki: (h, 0, 0, 0)),
            pl.BlockSpec((1, 2 * d), lambda b, h, qi, ki: (0, 0)),
        ],
        out_specs=pl.BlockSpec((None, tq, 2 * d), lambda b, h, qi, ki: (b, qi, h)),
        out_shape=jax.ShapeDtypeStruct((bsz, L, gw), BF16),
        scratch_shapes=[stat(LANES), stat(LANES), stat(2 * d), stat(LANES), stat(LANES), stat(2 * d)],
        compiler_params=_params(("parallel", "parallel", "parallel", "arbitrary")),
        name="diff_attention",
    )(lam.reshape(1), zq, zq, zkv, zkv, zkv, bias, subln_g.reshape(1, 2 * d))


def _mla_attn_kernel(q_ref, k_ref, v_ref, o_ref, m_ref, l_ref, acc_ref, *, strip):
    ki = pl.program_id(3)

    @pl.when(ki == 0)
    def _():
        _flash_init((m_ref, l_ref, acc_ref))

    k = k_ref[...]
    v = v_ref[...]
    for r in range(q_ref.shape[0] // strip):
        rows = slice(r * strip, (r + 1) * strip)
        _flash_strip(q_ref[rows, :], k, v, None, rows, m_ref, l_ref, acc_ref)

    @pl.when(ki == pl.num_programs(3) - 1)
    def _():
        o_ref[...] = (acc_ref[...] / _lanes(l_ref[...], acc_ref.shape[1])).astype(o_ref.dtype)


def mla_attention(q_cat, k_cat, kv):
    bsz, heads, L, dk = q_cat.shape
    tq, tk, strip = _attn_tiles(L)
    stat = lambda w: pltpu.VMEM((tq, w), F32)
    return pl.pallas_call(
        functools.partial(_mla_attn_kernel, strip=strip),
        grid=(bsz, heads, L // tq, L // tk),
        in_specs=[
            pl.BlockSpec((None, None, tq, dk), lambda b, h, qi, ki: (b, h, qi, 0)),
            pl.BlockSpec((None, None, tk, dk), lambda b, h, qi, ki: (b, h, ki, 0)),
            pl.BlockSpec((None, tk, MLA_V), lambda b, h, qi, ki: (b, ki, 2 * h + 1)),
        ],
        out_specs=pl.BlockSpec((None, tq, MLA_V), lambda b, h, qi, ki: (b, qi, h)),
        out_shape=jax.ShapeDtypeStruct((bsz, L, heads * MLA_V), F32),
        scratch_shapes=[stat(LANES), stat(LANES), stat(MLA_V)],
        compiler_params=_params(("parallel", "parallel", "parallel", "arbitrary")),
        name="mla_attention",
    )(q_cat, k_cat, kv)


def _mla_prep_kernel(qn_ref, qr_ref, qs_ref, kn_ref, ka_ref, kb_ref, cos_ref, sin_ref, qo_ref, ko_ref, *, qscale):
    h = pl.program_id(2)
    cos = cos_ref[...]
    sin = sin_ref[...]
    qrot = (qr_ref[...] * cos + qs_ref[...] * sin) * qscale
    lane = lax.broadcasted_iota(jnp.int32, qrot.shape, 1)
    lo = (h % 2) * MLA_ROPE
    mine = jnp.logical_and(lane >= lo, lane < lo + MLA_ROPE)
    qo_ref[:, :MLA_NOPE] = (qn_ref[...] * qscale).astype(qo_ref.dtype)
    qo_ref[:, MLA_NOPE:] = jnp.where(mine, qrot, 0.0).astype(qo_ref.dtype)
    ko_ref[:, :MLA_NOPE] = kn_ref[...]
    ko_ref[:, MLA_NOPE:] = (ka_ref[...] * cos + kb_ref[...] * sin).astype(ko_ref.dtype)


def mla_prep(qproj, kv, krr, cos2, sin2, heads, tm=512):
    bsz, L, _ = qproj.shape
    tm = _pick(L, tm)
    nq = heads * MLA_NOPE // LANES
    npair = heads * MLA_ROPE // LANES
    out = jax.ShapeDtypeStruct((bsz, heads, L, 2 * LANES), BF16)
    blk = lambda f: pl.BlockSpec((None, tm, LANES), f)
    return pl.pallas_call(
        functools.partial(_mla_prep_kernel, qscale=(MLA_NOPE + MLA_ROPE) ** -0.5 * LOG2E),
        grid=(bsz, L // tm, heads),
        in_specs=[
            blk(lambda b, i, h: (b, i, h)),
            blk(lambda b, i, h: (b, i, nq + h // 2)),
            blk(lambda b, i, h: (b, i, nq + npair + h // 2)),
            blk(lambda b, i, h: (b, i, 2 * h)),
            blk(lambda b, i, h: (b, i, 0)),
            blk(lambda b, i, h: (b, i, 1)),
            pl.BlockSpec((tm, LANES), lambda b, i, h: (i, 0)),
            pl.BlockSpec((tm, LANES), lambda b, i, h: (i, 0)),
        ],
        out_specs=[pl.BlockSpec((None, None, tm, 2 * LANES), lambda b, i, h: (b, h, i, 0))] * 2,
        out_shape=[out, out],
        compiler_params=_params(("parallel", "parallel", "arbitrary")),
        name="mla_prep",
    )(qproj, qproj, qproj, kv, krr, krr, cos2, sin2)


def _s5_dir_kernel(u_ref, wb_ref, wc_ref, cst_ref, y_ref, bu_ref, st_ref, carry_ref, *, reverse, nstate):
    c = pl.program_id(2)
    tc = st_ref.shape[0]
    ngrp = tc // SUBLANES

    @pl.when(c == 0)
    def _():
        carry_ref[...] = jnp.zeros_like(carry_ref)

    bu_ref[...] = jnp.dot(u_ref[...].astype(BF16), wb_ref[...], preferred_element_type=F32)

    for lt in range(nstate // LANES):
        re_sl = slice(lt * LANES, (lt + 1) * LANES)
        im_sl = slice(nstate + lt * LANES, nstate + (lt + 1) * LANES)
        pw = [(cst_ref[j, :, re_sl], cst_ref[j, :, im_sl]) for j in range(4)]

        def body(g, carry, re_sl=re_sl, im_sl=im_sl, pw=pw):
            cr, ci = carry
            r = (ngrp - 1 - g) if reverse else g
            rows = pl.ds(pl.multiple_of(r * SUBLANES, SUBLANES), SUBLANES)
            xr = bu_ref[rows, re_sl]
            xi = bu_ref[rows, im_sl]
            for j, k in enumerate((1, 2, 4)):
                ar, ai = pw[j]
                sh = (SUBLANES - k) if reverse else k
                sr = pltpu.roll(xr, sh, 0)
                si = pltpu.roll(xi, sh, 0)
                xr, xi = xr + ar * sr - ai * si, xi + ar * si + ai * sr
            pr, pi = pw[3]
            xr, xi = xr + pr * cr - pi * ci, xi + pr * ci + pi * cr
            st_ref[rows, re_sl] = xr
            st_ref[rows, im_sl] = xi
            edge = 0 if reverse else SUBLANES - 1
            cr = jnp.broadcast_to(xr[edge:edge + 1, :], xr.shape)
            ci = jnp.broadcast_to(xi[edge:edge + 1, :], xi.shape)
            return cr, ci

        carry0 = (carry_ref[:, re_sl], carry_ref[:, im_sl])
        cr, ci = lax.fori_loop(0, ngrp, body, carry0, unroll=4)
        carry_ref[:, re_sl] = cr
        carry_ref[:, im_sl] = ci

    y_ref[...] = jnp.dot(st_ref[...].astype(BF16), wc_ref[...], preferred_element_type=F32)


def _s5_weights(lam_re, lam_im, log_dt, b_re, b_im, c_re, c_im):
    ndir, G, P = lam_re.shape
    H = b_re.shape[-1]
    gps = LANES // H
    nslab = G // gps
    nstate = gps * P
    dt = jnp.exp(log_dt.astype(F32))[..., None]
    lr, li = lam_re.astype(F32), lam_im.astype(F32)
    mag = jnp.exp(lr * dt)
    abar_re, abar_im = mag * jnp.cos(li * dt), mag * jnp.sin(li * dt)
    den = lr * lr + li * li
    num_re = abar_re - 1.0
    coef_re = (num_re * lr + abar_im * li) / den
    coef_im = (abar_im * lr - num_re * li) / den
    br, bi = b_re.astype(F32), b_im.astype(F32)
    bbar_re = coef_re[..., None] * br - coef_im[..., None] * bi
    bbar_im = coef_re[..., None] * bi + coef_im[..., None] * br
    eye = jnp.eye(gps, dtype=F32)

    def blockdiag_in(w):
        w = w.reshape(ndir, nslab, gps, P, H)
        return jnp.einsum('dsgph,gk->dsghkp', w, eye).reshape(ndir, nslab, gps * H, gps * P)

    def blockdiag_out(w):
        w = w.reshape(ndir, nslab, gps, H, P)
        return jnp.einsum('dsghp,gk->dsgpkh', w, eye).reshape(ndir, nslab, gps * P, gps * H)

    wb = jnp.concatenate([blockdiag_in(bbar_re), blockdiag_in(bbar_im)], axis=-1).astype(BF16)
    wc = jnp.concatenate([blockdiag_out(c_re.astype(F32)), -blockdiag_out(c_im.astype(F32))], axis=-2).astype(BF16)
    pr, pi = [abar_re], [abar_im]
    for _ in range(SUBLANES - 1):
        pr, pi = pr + [pr[-1] * abar_re - pi[-1] * abar_im], pi + [pr[-1] * abar_im + pi[-1] * abar_re]
    pr = jnp.stack(pr).reshape(SUBLANES, ndir, nslab, nstate)
    pi = jnp.stack(pi).reshape(SUBLANES, ndir, nslab, nstate)
    row = jnp.arange(SUBLANES)
    consts = []
    for dirn in range(ndir):
        tabs = []
        for k in (1, 2, 4):
            keep = (row >= k) if dirn == 0 else (row < SUBLANES - k)
            tabs.append(jnp.concatenate([jnp.where(keep[:, None, None], pr[k - 1, dirn][None], 0.0),
                                         jnp.where(keep[:, None, None], pi[k - 1, dirn][None], 0.0)], axis=-1))
        order = row if dirn == 0 else row[::-1]
        tabs.append(jnp.concatenate([pr[order, dirn], pi[order, dirn]], axis=-1))
        consts.append(jnp.stack(tabs))
    cst = jnp.stack(consts).transpose(0, 3, 1, 2, 4)
    return wb, wc, cst, nstate


def s5_scan(u, wb, wc, cst, nstate, reverse):
    bsz, L, gw = u.shape
    tc = _pick(L, SCAN_CHUNK)
    nc = L // tc
    nslab = gw // LANES
    cidx = (lambda c: nc - 1 - c) if reverse else (lambda c: c)
    kernel = functools.partial(_s5_dir_kernel, reverse=reverse, nstate=nstate)
    return pl.pallas_call(
        kernel,
        grid=(bsz, nslab, nc),
        in_specs=[
            pl.BlockSpec((None, tc, LANES), lambda b, s, c: (b, cidx(c), s)),
            pl.BlockSpec((None, LANES, 2 * nstate), lambda b, s, c: (s, 0, 0)),
            pl.BlockSpec((None, 2 * nstate, LANES), lambda b, s, c: (s, 0, 0)),
            pl.BlockSpec((None, 4, SUBLANES, 2 * nstate), lambda b, s, c: (s, 0, 0, 0)),
        ],
        out_specs=pl.BlockSpec((None, tc, LANES), lambda b, s, c: (b, cidx(c), s)),
        out_shape=jax.ShapeDtypeStruct((bsz, L, gw), F32),
        scratch_shapes=[pltpu.VMEM((tc, 2 * nstate), F32), pltpu.VMEM((tc, 2 * nstate), F32),
                        pltpu.VMEM((SUBLANES, 2 * nstate), F32)],
        compiler_params=_params(("parallel", "parallel", "arbitrary")),
        name="s5_scan_bwd" if reverse else "s5_scan_fwd",
    )(u, wb, wc, cst)


def _s5_post_kernel(u_ref, yf_ref, yb_ref, d_ref, w_ref, g_ref, o_ref):
    y = u_ref[...] * d_ref[...] + yf_ref[...] + yb_ref[...]
    g = _gelu(y)
    gate = jnp.dot(g.astype(BF16), w_ref[...], preferred_element_type=F32)
    out = g * _sigmoid(gate)
    ms = jnp.mean(out * out, axis=-1, keepdims=True)
    o_ref[...] = (out * lax.rsqrt(ms + EPS) * g_ref[...]).astype(o_ref.dtype)


def s5_post(u, yf, yb, d_skip, w_glu, g_group, tm=512):
    m, gw = u.shape
    tm = _pick(m, tm)
    row = pl.BlockSpec((tm, gw), lambda i: (i, 0))
    vec = pl.BlockSpec((1, gw), lambda i: (0, 0))
    return pl.pallas_call(
        _s5_post_kernel,
        grid=(m // tm,),
        in_specs=[row, row, row, vec, pl.BlockSpec((gw, gw), lambda i: (0, 0)), vec],
        out_specs=row,
        out_shape=jax.ShapeDtypeStruct((m, gw), BF16),
        compiler_params=_params(("parallel",)),
        name="s5_post",
    )(u, yf, yb, d_skip.reshape(1, gw), w_glu, g_group.reshape(1, gw))


def _shift_rows(x, halo, k, valid):
    sh = pltpu.roll(x, k, 0)
    hs = pltpu.roll(jnp.where(valid, halo, 0.0), k, 0)
    row = lax.broadcasted_iota(jnp.int32, hs.shape, 0)
    head = jnp.where(row < k, hs, sh[:SUBLANES])
    return jnp.concatenate([head, sh[SUBLANES:]], axis=0)


def _shift_rows_up(x, halo, k, valid):
    t = x.shape[0]
    sh = pltpu.roll(x, t - k, 0)
    hs = pltpu.roll(jnp.where(valid, halo, 0.0), SUBLANES - k, 0)
    row = lax.broadcasted_iota(jnp.int32, hs.shape, 0)
    tail = jnp.where(row >= SUBLANES - k, hs, sh[t - SUBLANES:])
    return jnp.concatenate([sh[:t - SUBLANES], tail], axis=0)


def _lru_dir_kernel(x_ref, xp_ref, xn_ref, cw_ref, cb_ref, w_ref, bg_ref, lam_ref, h_ref,
                    a_s, b_s, carry_ref, *, reverse, blk):
    c = pl.program_id(1)
    nc = pl.num_programs(1)
    tc, gw = x_ref.shape
    ngrp = tc // SUBLANES
    pos = (nc - 1 - c) if reverse else c

    @pl.when(c == 0)
    def _():
        carry_ref[...] = jnp.zeros_like(carry_ref)

    x = x_ref[...]
    has_prev = pos > 0
    has_next = pos < nc - 1
    xc = (_shift_rows(x, xp_ref[...], 2, has_prev) * cw_ref[0:1, :]
          + _shift_rows(x, xp_ref[...], 1, has_prev) * cw_ref[1:2, :]
          + x * cw_ref[2:3, :]
          + _shift_rows_up(x, xn_ref[...], 1, has_next) * cw_ref[3:4, :]) + cb_ref[...]
    xcb = xc.astype(BF16)
    lam = lam_ref[...]
    log_sig = jnp.minimum(lam, 0.0) - jnp.log(1.0 + jnp.exp(-jnp.abs(lam)))
    for n in range(gw // blk):
        cols = slice(n * blk, (n + 1) * blk)
        pre = jnp.dot(xcb[:, cols], w_ref[n], preferred_element_type=F32)
        r = _sigmoid(pre[:, :blk] + bg_ref[0:1, cols])
        i = _sigmoid(pre[:, blk:] + bg_ref[1:2, cols])
        a = jnp.exp(LRU_C * r * log_sig[:, cols])
        a_s[:, cols] = a
        b_s[:, cols] = jnp.sqrt(1.0 - a * a) * (i * xc[:, cols])

    row = lax.broadcasted_iota(jnp.int32, (SUBLANES, LANES), 0)
    for lt in range(gw // LANES):
        cols = slice(lt * LANES, (lt + 1) * LANES)

        def body(g, carry, cols=cols):
            r = (ngrp - 1 - g) if reverse else g
            rows = pl.ds(pl.multiple_of(r * SUBLANES, SUBLANES), SUBLANES)
            a = a_s[rows, cols]
            b = b_s[rows, cols]
            for k in (1, 2, 4):
                sh = (SUBLANES - k) if reverse else k
                keep = (row < SUBLANES - k) if reverse else (row >= k)
                b = b + a * jnp.where(keep, pltpu.roll(b, sh, 0), 0.0)
                a = a * jnp.where(keep, pltpu.roll(a, sh, 0), 1.0)
            hval = b + a * carry
            h_ref[rows, cols] = hval
            edge = 0 if reverse else SUBLANES - 1
            return jnp.broadcast_to(hval[edge:edge + 1, :], hval.shape)

        carry_ref[:, cols] = lax.fori_loop(0, ngrp, body, carry_ref[:, cols], unroll=4)


def lru_scan(zc, conv_w, conv_b, w_gates, b_gates, lam, reverse):
    bsz, L, w2 = zc.shape
    gw = w2 // 2
    nblk, blk = w_gates.shape[1], w_gates.shape[2]
    tc = _pick(L, SCAN_CHUNK)
    nc = L // tc
    hb = tc // SUBLANES
    cidx = (lambda c: nc - 1 - c) if reverse else (lambda c: c)
    wcat = jnp.concatenate([w_gates[0], w_gates[1]], axis=-1).astype(BF16)
    kernel = functools.partial(_lru_dir_kernel, reverse=reverse, blk=blk)
    vec = lambda r: pl.BlockSpec((r, gw), lambda b, c: (0, 0))
    return pl.pallas_call(
        kernel,
        grid=(bsz, nc),
        in_specs=[
            pl.BlockSpec((None, tc, gw), lambda b, c: (b, cidx(c), 0)),
            pl.BlockSpec((None, SUBLANES, gw), lambda b, c: (b, jnp.maximum(cidx(c) * hb - 1, 0), 0)),
            pl.BlockSpec((None, SUBLANES, gw), lambda b, c: (b, jnp.minimum((cidx(c) + 1) * hb, L // SUBLANES - 1), 0)),
            vec(4), vec(1),
            pl.BlockSpec((nblk, blk, 2 * blk), lambda b, c: (0, 0, 0)),
            vec(2), vec(1),
        ],
        out_specs=pl.BlockSpec((None, tc, gw), lambda b, c: (b, cidx(c), 0)),
        out_shape=jax.ShapeDtypeStruct((bsz, L, gw), F32),
        scratch_shapes=[pltpu.VMEM((tc, gw), F32), pltpu.VMEM((tc, gw), F32), pltpu.VMEM((SUBLANES, gw), F32)],
        compiler_params=_params(("parallel", "arbitrary")),
        name="lru_scan_bwd" if reverse else "lru_scan_fwd",
    )(zc, zc, zc, conv_w, conv_b.reshape(1, gw), wcat, b_gates, lam.reshape(1, gw))


def _lru_post_kernel(hf_ref, hb_ref, xg_ref, g_ref, o_ref):
    out = (hf_ref[...] + hb_ref[...]) * _gelu(xg_ref[...])
    ms = jnp.mean(out * out, axis=-1, keepdims=True)
    o_ref[...] = (out * lax.rsqrt(ms + EPS) * g_ref[...]).astype(o_ref.dtype)


def lru_post(hf, hb, zc, g_group, tm=512):
    m, gw = hf.shape
    tm = _pick(m, tm)
    row = pl.BlockSpec((tm, gw), lambda i: (i, 0))
    return pl.pallas_call(
        _lru_post_kernel,
        grid=(m // tm,),
        in_specs=[row, row, pl.BlockSpec((tm, gw), lambda i: (i, 1)), pl.BlockSpec((1, gw), lambda i: (0, 0))],
        out_specs=row,
        out_shape=jax.ShapeDtypeStruct((m, gw), BF16),
        compiler_params=_params(("parallel",)),
        name="lru_post",
    )(hf, hb, zc, g_group.reshape(1, gw))


def _ffn_down_kernel(g_ref, gp_ref, gn_ref, v_ref, vp_ref, vn_ref, wg_ref, wv_ref, bg_ref, bv_ref, wd_ref, o_ref):
    i = pl.program_id(1)
    j = pl.program_id(2)
    has_prev = i > 0
    has_next = i < pl.num_programs(1) - 1

    def conv(x_ref, p_ref, n_ref, w_ref, b_ref, cols):
        x = x_ref[:, cols].astype(F32)
        prev = p_ref[:, cols].astype(F32)[-SUBLANES:]
        nxt = n_ref[:, cols].astype(F32)[:SUBLANES]
        return (_shift_rows(x, prev, 1, has_prev) * w_ref[0:1, cols]
                + x * w_ref[1:2, cols]
                + _shift_rows_up(x, nxt, 1, has_next) * w_ref[2:3, cols]) + b_ref[:, cols]

    @pl.when(j == 0)
    def _():
        o_ref[...] = jnp.zeros_like(o_ref)

    for c in range(g_ref.shape[1] // FFN_SUB):
        cols = slice(c * FFN_SUB, (c + 1) * FFN_SUB)
        gate = conv(g_ref, gp_ref, gn_ref, wg_ref, bg_ref, cols)
        val = conv(v_ref, vp_ref, vn_ref, wv_ref, bv_ref, cols)
        act = (_gelu(gate) * val).astype(BF16)
        o_ref[...] += jnp.dot(act, wd_ref[cols, :], preferred_element_type=F32)


def ffn_down(up, conv_w, conv_b, w_down):
    bsz, L, w2 = up.shape
    dff = w2 // 2
    d = w_down.shape[1]
    tm, tc = _pick(L, FFN_ROWS), _pick(dff, FFN_COLS)
    hrows = 2 * SUBLANES
    hb = tm // hrows
    nj = dff // tc
    nh = L // hrows
    main = lambda off: pl.BlockSpec((None, tm, tc), lambda b, i, j: (b, i, off + j))
    prev = lambda off: pl.BlockSpec((None, hrows, tc), lambda b, i, j: (b, jnp.maximum(i * hb - 1, 0), off + j))
    nxt = lambda off: pl.BlockSpec((None, hrows, tc), lambda b, i, j: (b, jnp.minimum((i + 1) * hb, nh - 1), off + j))
    wspec = lambda off: pl.BlockSpec((3, tc), lambda b, i, j: (0, off + j))
    bspec = lambda off: pl.BlockSpec((1, tc), lambda b, i, j: (0, off + j))
    return pl.pallas_call(
        _ffn_down_kernel,
        grid=(bsz, L // tm, nj),
        in_specs=[main(0), prev(0), nxt(0), main(nj), prev(nj), nxt(nj),
                  wspec(0), wspec(nj), bspec(0), bspec(nj),
                  pl.BlockSpec((tc, d), lambda b, i, j: (j, 0))],
        out_specs=pl.BlockSpec((None, tm, d), lambda b, i, j: (b, i, 0)),
        out_shape=jax.ShapeDtypeStruct((bsz, L, d), F32),
        compiler_params=_params(("parallel", "parallel", "arbitrary")),
        name="ffn_down",
    )(up, up, up, up, up, up, conv_w, conv_w, conv_b.reshape(1, w2), conv_b.reshape(1, w2), w_down)


def _prep_layer(l, p):
    d = p['w_in'].shape[1]
    gw = d // 4
    q_rank = p['mla_q_norm'].shape[1]
    kv_rank = p['mla_kv_norm'].shape[1]
    heads = gw // MLA_V
    o = [3 * gw, 4 * gw, 6 * gw, 6 * gw + q_rank, 6 * gw + q_rank + kv_rank]
    w_in = p['w_in'][l]
    half = MLA_ROPE // 2
    swap = jnp.concatenate([jnp.arange(half, MLA_ROPE), jnp.arange(half)])
    w_kr = w_in[:, o[4]:]
    w_kr_s = w_kr[:, swap]
    w_uq = p['mla_w_uq'][l].reshape(q_rank, heads, MLA_NOPE + MLA_ROPE)
    uq_nope = w_uq[:, :, :MLA_NOPE].reshape(q_rank, heads * MLA_NOPE)
    uq_rope = w_uq[:, :, MLA_NOPE:]
    uq_rope_s = uq_rope[:, :, swap]
    lf = p['diff_lambda'][l].astype(F32)
    lam_init = 0.8 - 0.6 * math.exp(-0.3 * l)
    wb, wc, cst, nstate = _s5_weights(p['s5_lambda_re'][l], p['s5_lambda_im'][l], p['s5_log_dt'][l],
                                      p['s5_b_re'][l], p['s5_b_im'][l], p['s5_c_re'][l], p['s5_c_im'][l])
    return dict(
        lam_init=lam_init,
        lam=jnp.exp(jnp.sum(lf[0] * lf[1])) - jnp.exp(jnp.sum(lf[2] * lf[3])) + lam_init,
        w_aq=w_in[:, :gw].astype(BF16), w_akv=w_in[:, gw:o[0]].astype(BF16), w_b=w_in[:, o[0]:o[1]].astype(BF16),
        w_c=w_in[:, o[1]:o[2]].astype(BF16), w_cq=w_in[:, o[2]:o[3]].astype(BF16),
        w_ckv=w_in[:, o[3]:o[4]].astype(BF16),
        w_kr=jnp.concatenate([w_kr, w_kr, w_kr_s, w_kr_s], axis=1).astype(BF16),
        w_uq=jnp.concatenate([uq_nope, uq_rope.reshape(q_rank, -1), uq_rope_s.reshape(q_rank, -1)], axis=1).astype(BF16),
        w_ukv=p['mla_w_ukv'][l].astype(BF16),
        s5_wb=wb, s5_wc=wc, s5_cst=cst, s5_nstate=nstate,
        w_glu=p['s5_w_glu'][l].astype(BF16),
        w_out=p['w_out'][l].astype(BF16), w_up=p['w_up'][l].astype(BF16), w_down=p['w_down'][l].astype(BF16),
    )


def _rope_tables(L):
    half = MLA_ROPE // 2
    inv = ROPE_BASE ** (-jnp.arange(half, dtype=F32) / half)
    ang = jnp.arange(L).astype(F32)[:, None] * inv[None, :]
    cos, sin = jnp.cos(ang), jnp.sin(ang)
    reps = LANES // MLA_ROPE
    return jnp.tile(jnp.concatenate([cos, cos], axis=1), (1, reps)), jnp.tile(jnp.concatenate([-sin, sin], axis=1), (1, reps))


def _layer(x, h, l, p, w, g_next, bsz, L):
    m, d = x.shape
    gw = d // 4
    heads = gw // MLA_V
    b3 = lambda a: a.reshape(bsz, L, a.shape[-1])
    f2 = lambda a: a.reshape(m, a.shape[-1])
    zq = matmul(h, w['w_aq'], BF16, scale=DIFF_HEAD_DIM ** -0.5 * LOG2E)
    zkv = matmul(h, w['w_akv'], BF16)
    ya = f2(diff_attention(b3(zq), b3(zkv), w['lam'], p['diff_subln'][l], p['rel_bias'], w['lam_init']))
    zb = matmul(h, w['w_b'], F32)
    yf = s5_scan(b3(zb), w['s5_wb'][0], w['s5_wc'][0], w['s5_cst'][0], w['s5_nstate'], False)
    yr = s5_scan(b3(zb), w['s5_wb'][1], w['s5_wc'][1], w['s5_cst'][1], w['s5_nstate'], True)
    yb = s5_post(zb, f2(yf), f2(yr), p['s5_d'][l], w['w_glu'], p['g_group'][l, 0])
    zc = matmul(h, w['w_c'], F32)
    hf = lru_scan(b3(zc), p['lru_conv_w'][l], p['lru_conv_b'][l], p['lru_w_gates'][l, 0],
                  p['lru_b_gates'][l, 0], p['lru_lambda'][l, 0], False)
    hr = lru_scan(b3(zc), p['lru_conv_w'][l], p['lru_conv_b'][l], p['lru_w_gates'][l, 1],
                  p['lru_b_gates'][l, 1], p['lru_lambda'][l, 1], True)
    yc = lru_post(f2(hf), f2(hr), zc, p['g_group'][l, 1])
    cq = rms_cast(matmul(h, w['w_cq'], F32), p['mla_q_norm'][l])
    ckv = rms_cast(matmul(h, w['w_ckv'], F32), p['mla_kv_norm'][l])
    qproj = matmul(cq, w['w_uq'], F32)
    kv = matmul(ckv, w['w_ukv'], BF16)
    krr = matmul(h, w['w_kr'], F32)
    cos2, sin2 = _rope_tables(L)
    q_cat, k_cat = mla_prep(b3(qproj), b3(kv), b3(krr), cos2, sin2, heads)
    yd = rms_cast(f2(mla_attention(q_cat, k_cat, b3(kv))), p['g_group'][l, 2])
    mix = matmul(jnp.concatenate([ya, yb, yc, yd], axis=1), w['w_out'], F32)
    x, h2 = resid_norm(x, mix, p['g_norms'][l, 1], p['g_norms'][l, 2])
    up = matmul(h2, w['w_up'], BF16)
    f = ffn_down(b3(up), p['ffn_conv_w'][l], p['ffn_conv_b'][l], w['w_down'])
    return resid_norm(x, f2(f), p['g_norms'][l, 3], g_next)


def _trunk(x3, p, ws):
    bsz, L, d = x3.shape
    depth = len(ws)
    x = x3.reshape(bsz * L, d)
    h = rms_cast(x, p['g_norms'][0, 0])
    for l in range(depth):
        g_next = p['g_norms'][l + 1, 0] if l + 1 < depth else p['g_norms'][l, 0]
        x, h = _layer(x, h, l, p, ws[l], g_next, bsz, L)
    return x.reshape(bsz, L, d)


def kernel(x_prompt, x_sample, g_norms, w_in, diff_lambda, diff_subln, rel_bias, s5_lambda_re, s5_lambda_im, s5_log_dt, s5_b_re, s5_b_im, s5_c_re, s5_c_im, s5_d, s5_w_glu, lru_conv_w, lru_conv_b, lru_w_gates, lru_b_gates, lru_lambda, mla_q_norm, mla_w_uq, mla_kv_norm, mla_w_ukv, g_group, w_out, w_up, ffn_conv_w, ffn_conv_b, w_down):
    p = {
        'g_norms': g_norms, 'w_in': w_in, 'diff_lambda': diff_lambda, 'diff_subln': diff_subln,
        'rel_bias': rel_bias, 's5_lambda_re': s5_lambda_re, 's5_lambda_im': s5_lambda_im,
        's5_log_dt': s5_log_dt, 's5_b_re': s5_b_re, 's5_b_im': s5_b_im, 's5_c_re': s5_c_re,
        's5_c_im': s5_c_im, 's5_d': s5_d, 's5_w_glu': s5_w_glu, 'lru_conv_w': lru_conv_w,
        'lru_conv_b': lru_conv_b, 'lru_w_gates': lru_w_gates, 'lru_b_gates': lru_b_gates,
        'lru_lambda': lru_lambda, 'mla_q_norm': mla_q_norm, 'mla_w_uq': mla_w_uq,
        'mla_kv_norm': mla_kv_norm, 'mla_w_ukv': mla_w_ukv, 'g_group': g_group, 'w_out': w_out,
        'w_up': w_up, 'ffn_conv_w': ffn_conv_w, 'ffn_conv_b': ffn_conv_b, 'w_down': w_down,
    }
    ws = [_prep_layer(l, p) for l in range(w_in.shape[0])]
    return (_trunk(x_prompt, p, ws), _trunk(x_sample, p, ws))
```
